```python
import jax, jax.numpy as jnp
from jax import lax
import numpy as np

D_MODEL = 1024
BATCH = 1
SEQ = 16384
DEPTH = 1
DEC_BATCH = 32
DEC_SEQ = 2048
PAST_LEN = 128

GRID_W = 64
M_HEADS = 4
M_HEAD_DIM = 256
M_WIDTH = M_HEADS * M_HEAD_DIM
M_CHUNK = 64
CONV_W = 5
N_GATES = 4 * M_HEADS
NA_HEADS = 16
NA_HEAD_DIM = 64
NA_WIDTH = NA_HEADS * NA_HEAD_DIM
NA_MAX_KH = 8
NA_KW = 16
NA_QCB = 16
NA_KCB = 32
SPLITS = (M_WIDTH, M_WIDTH, M_WIDTH, M_WIDTH, M_WIDTH, N_GATES, NA_WIDTH, NA_WIDTH, NA_WIDTH, NA_WIDTH, D_MODEL, D_MODEL)
D_IN = 5 * M_WIDTH + N_GATES + 4 * NA_WIDTH + 2 * D_MODEL
EPS = 1e-6
NEG = -1e30

kernel_name = 'hybrid_mlstm_natten_encoder'


def _rmsnorm(x, w):
    x32 = x.astype(jnp.float32)
    y = x32 * lax.rsqrt(jnp.mean(x32 * x32, axis=-1, keepdims=True) + EPS)
    return (y * w.astype(jnp.float32)).astype(x.dtype)


def _centred_conv(u, w, b):
    T = u.shape[1]
    pad = CONV_W // 2
    up = jnp.pad(u, ((0, 0), (pad, pad), (0, 0)))
    out = b
    for i in range(CONV_W):
        out = out + up[:, i:i + T] * w[i]
    return out


def _mlstm_scan(q, k, v, li, lf):
    B, H, T, dk = q.shape
    dv = v.shape[-1]
    nc = T // M_CHUNK

    def chunks(a):
        return jnp.moveaxis(a.reshape((B, H, nc, M_CHUNK) + a.shape[3:]), 2, 0)

    qc, kc, vc, lic = chunks(q), chunks(k), chunks(v), chunks(li)
    bc = jnp.cumsum(chunks(lf), axis=-1)
    tril = jnp.tril(jnp.ones((M_CHUNK, M_CHUNK), dtype=bool))

    def step(carry, xs):
        S, n, m = carry
        q_, k_, v_, li_, b_ = xs
        dmat = jnp.where(tril, b_[..., :, None] - b_[..., None, :] + li_[..., None, :], -jnp.inf)
        inter = b_ + m[..., None]
        mj = jnp.maximum(inter, jnp.max(dmat, axis=-1))
        w_int = jnp.exp(inter - mj)
        p = jnp.einsum('bhld,bhsd->bhls', q_, k_) * jnp.exp(dmat - mj[..., None])
        num = w_int[..., None] * jnp.einsum('bhld,bhde->bhle', q_, S) + jnp.einsum('bhls,bhse->bhle', p, v_)
        den = w_int * jnp.einsum('bhld,bhd->bhl', q_, n) + jnp.sum(p, axis=-1)
        h = num / jnp.maximum(jnp.abs(den), jnp.exp(-mj))[..., None]
        b_last = b_[..., -1]
        g = b_last[..., None] - b_ + li_
        m_new = jnp.maximum(b_last + m, jnp.max(g, axis=-1))
        a = jnp.exp(b_last + m - m_new)
        kw = k_ * jnp.exp(g - m_new[..., None])[..., None]
        S_new = a[..., None, None] * S + jnp.einsum('bhld,bhle->bhde', kw, v_)
        n_new = a[..., None] * n + jnp.sum(kw, axis=2)
        return (S_new, n_new, m_new), h

    init = (jnp.zeros((B, H, dk, dv), jnp.float32), jnp.zeros((B, H, dk), jnp.float32),
            jnp.zeros((B, H), jnp.float32))
    _, h = lax.scan(step, init, (qc, kc, vc, lic, bc))
    return jnp.moveaxis(h, 0, 2).reshape(B, H, T, dv)


def _mlstm_branch(q_pre, k_pre, v_in, o_pre, gate_pre, b_gate, conv_w, conv_b, mh_w):
    B, T, _ = q_pre.shape
    qk = jax.nn.silu(_centred_conv(jnp.concatenate([q_pre, k_pre], axis=-1), conv_w, conv_b))
    q, k = jnp.split(qk, 2, axis=-1)

    def heads(a):
        return a.astype(jnp.float32).reshape(B, T, M_HEADS, M_HEAD_DIM).transpose(0, 2, 1, 3)

    q, k, v = heads(q), heads(k) * (M_HEAD_DIM ** -0.5), heads(v_in)
    g = (gate_pre.astype(jnp.float32) + b_gate.astype(jnp.float32)).transpose(0, 2, 1)
    ig_f, fg_f, ig_b, fg_b = jnp.split(g, 4, axis=1)
    h_f = _mlstm_scan(q, k, v, ig_f, jax.nn.log_sigmoid(fg_f))

    def flip(a):
        return jnp.flip(a, axis=2)

    h_b = flip(_mlstm_scan(flip(q), flip(k), flip(v), flip(ig_b), flip(jax.nn.log_sigmoid(fg_b))))
    h = (h_f + h_b).transpose(0, 2, 1, 3)
    h = jax.nn.sigmoid(o_pre.astype(jnp.float32)).reshape(B, T, M_HEADS, M_HEAD_DIM) * h
    h = h * lax.rsqrt(jnp.mean(h * h, axis=-1, keepdims=True) + EPS) \
        * mh_w.astype(jnp.float32).reshape(M_HEADS, M_HEAD_DIM)
    return h.reshape(B, T, M_WIDTH).astype(q_pre.dtype)


def _na_col_tables():
    c = np.arange(GRID_W)
    cs = np.clip(c - NA_KW // 2, 0, GRID_W - NA_KW)
    ncb = GRID_W // NA_QCB
    c0 = np.arange(ncb) * NA_QCB
    kb = np.clip(c0 - NA_KW // 2, 0, GRID_W - NA_KCB)
    kcol = kb[:, None] + np.arange(NA_KCB)[None, :]
    qcol = c.reshape(ncb, NA_QCB)
    dc = kcol[:, None, :] - qcol[:, :, None]
    qs = cs.reshape(ncb, NA_QCB)[:, :, None]
    valid = (kcol[:, None, :] >= qs) & (kcol[:, None, :] < qs + NA_KW)
    return kcol, dc, valid


def _neighbourhood_attention(q, k, v, rpb):
    B, T, _ = q.shape
    rows = T // GRID_W
    kh = min(NA_MAX_KH, rows)
    ncb = GRID_W // NA_QCB
    kcol, dc, valid = _na_col_tables()
    qg = q.reshape(B, rows, ncb, NA_QCB, NA_HEADS, NA_HEAD_DIM)
    kg = k.reshape(B, rows, GRID_W, NA_HEADS, NA_HEAD_DIM)
    vg = v.reshape(B, rows, GRID_W, NA_HEADS, NA_HEAD_DIM)
    rpb_c = rpb[:, :, np.clip(dc + NA_KW - 1, 0, 2 * NA_KW - 2)]
    valid_b = jnp.asarray(valid)[None, None, :, :, None, :]
    scale = NA_HEAD_DIM ** -0.5

    def row(r):
        rs = jnp.clip(r - kh // 2, 0, rows - kh)
        k_blk = lax.dynamic_slice_in_dim(kg, rs, kh, axis=1)[:, :, kcol]
        v_blk = lax.dynamic_slice_in_dim(vg, rs, kh, axis=1)[:, :, kcol]
        q_r = lax.dynamic_index_in_dim(qg, r, axis=1, keepdims=False)
        s = jnp.einsum('bnqhd,bknjhd->bhnqkj', q_r, k_blk).astype(jnp.float32) * scale
        dr = rs + jnp.arange(kh) - r + (NA_MAX_KH - 1)
        bias = jnp.take(rpb_c, dr, axis=1).transpose(0, 2, 3, 1, 4)
        s = jnp.where(valid_b, s + bias.astype(jnp.float32)[None], NEG)
        shp = s.shape
        p = jax.nn.softmax(s.reshape(shp[:4] + (kh * NA_KCB,)), axis=-1).reshape(shp).astype(v.dtype)
        o = jnp.einsum('bhnqkj,bknjhd->bnqhd', p, v_blk)
        return o.reshape(B, GRID_W, NA_WIDTH)

    out = lax.map(row, jnp.arange(rows))
    return out.transpose(1, 0, 2, 3).reshape(B, T, NA_WIDTH)


def _layer(x, norm_w, w_in, b_gate, conv_w, conv_b, mh_norm_w, rpb, w_down_a, w_down_b, w_out):
    xn = _rmsnorm(x, norm_w)
    idx = np.cumsum(SPLITS)[:-1]
    m_q, m_k, m_v, m_o, m_z, m_g, n_q, n_k, n_v, n_z, g_a, g_b = [xn @ w for w in jnp.split(w_in, idx, axis=1)]
    y_a = _mlstm_branch(m_q, m_k, m_v, m_o, m_g, b_gate, conv_w, conv_b, mh_norm_w) * jax.nn.silu(m_z)
    y_b = _neighbourhood_attention(n_q, n_k, n_v, rpb) * jax.nn.silu(n_z)
    merged = jax.nn.sigmoid(g_a) * (y_a @ w_down_a) + jax.nn.sigmoid(g_b) * (y_b @ w_down_b)
    return x + merged @ w_out


def _trunk(x, norm_w, w_in, b_gate, conv_w, conv_b, mh_norm_w, rpb, w_down_a, w_down_b, w_out, final_norm_w):
    for l in range(DEPTH):
        x = _layer(x, norm_w[l], w_in[l], b_gate[l], conv_w[l], conv_b[l], mh_norm_w[l], rpb[l],
                   w_down_a[l], w_down_b[l], w_out[l])
    return _rmsnorm(x, final_norm_w)


def setup_inputs(seed: int = 0) -> dict:
    key = jax.random.key(seed)
    ks = jax.random.split(key, 13)
    nrm = jax.random.normal
    fb = jnp.linspace(3.0, 6.0, M_HEADS)
    zb = jnp.zeros((M_HEADS,), jnp.float32)
    gate_base = jnp.concatenate([zb, fb, zb, fb])
    return {
        'x_prompt': nrm(ks[0], (BATCH, SEQ, D_MODEL), jnp.float32),
        'x_sample': nrm(ks[1], (DEC_BATCH, DEC_SEQ, D_MODEL), jnp.float32),
        'norm_w': 1.0 + 0.02 * nrm(ks[2], (DEPTH, D_MODEL), jnp.float32),
        'w_in': nrm(ks[3], (DEPTH, D_MODEL, D_IN), jnp.float32) * D_MODEL ** -0.5,
        'b_gate': gate_base[None, :] + 0.1 * nrm(ks[4], (DEPTH, N_GATES), jnp.float32),
        'conv_w': nrm(ks[5], (DEPTH, CONV_W, 2 * M_WIDTH), jnp.float32) * CONV_W ** -0.5,
        'conv_b': 0.01 * nrm(ks[6], (DEPTH, 2 * M_WIDTH), jnp.float32),
        'mh_norm_w': 1.0 + 0.02 * nrm(ks[7], (DEPTH, M_WIDTH), jnp.float32),
        'rpb': 0.1 * nrm(ks[8], (DEPTH, NA_HEADS, 2 * NA_MAX_KH - 1, 2 * NA_KW - 1), jnp.float32),
        'w_down_a': nrm(ks[9], (DEPTH, M_WIDTH, D_MODEL), jnp.float32) * M_WIDTH ** -0.5,
        'w_down_b': nrm(ks[10], (DEPTH, NA_WIDTH, D_MODEL), jnp.float32) * NA_WIDTH ** -0.5,
        'w_out': nrm(ks[11], (DEPTH, D_MODEL, D_MODEL), jnp.float32) * D_MODEL ** -0.5,
        'final_norm_w': 1.0 + 0.02 * nrm(ks[12], (D_MODEL,), jnp.float32),
    }


def reference(x_prompt, x_sample, norm_w, w_in, b_gate, conv_w, conv_b, mh_norm_w, rpb,
              w_down_a, w_down_b, w_out, final_norm_w):
    y_prompt = _trunk(x_prompt, norm_w, w_in, b_gate, conv_w, conv_b, mh_norm_w, rpb,
                      w_down_a, w_down_b, w_out, final_norm_w)
    y_sample = _trunk(x_sample, norm_w, w_in, b_gate, conv_w, conv_b, mh_norm_w, rpb,
                      w_down_a, w_down_b, w_out, final_norm_w)
    return (y_prompt, y_sample)
```

```python
import functools

import numpy as np
import jax
import jax.numpy as jnp
from jax import lax
from jax.experimental import pallas as pl
from jax.experimental.pallas import tpu as pltpu

D_MODEL = 1024
GRID_W = 64
M_HEADS = 4
M_HEAD_DIM = 256
M_WIDTH = M_HEADS * M_HEAD_DIM
CONV_W = 5
N_GATES = 4 * M_HEADS
NA_HEADS = 16
NA_HEAD_DIM = 64
NA_WIDTH = NA_HEADS * NA_HEAD_DIM
NA_MAX_KH = 8
NA_KW = 16
NA_QCB = 16
NA_KCB = 32
EPS = 1e-6
NEG = -1e30

LANES = 128
BF16_SUBLANES = 16
VMEM_LIMIT_BYTES = 56 * 1024 * 1024

COL_MQ, COL_MK, COL_MV, COL_MO, COL_MZ, COL_NQ, COL_NK, COL_NV, COL_NZ, COL_GA, COL_GB = range(11)
N_COL_TILES = 11
D_MAIN = N_COL_TILES * D_MODEL

PROJ_TM = 1024
HALO = BF16_SUBLANES
MLSTM_L = 256
NA_RB = 8
NA_WIN = 16
NA_HPB = 4
NA_LANES = NA_HPB * NA_HEAD_DIM
OUT_TM = 512


def _dot(a, b):
    return jnp.dot(a, b, preferred_element_type=jnp.float32)


def _dot_nt(a, b):
    return lax.dot_general(a, b, (((1,), (1,)), ((), ())), preferred_element_type=jnp.float32)


def _dot_tn(a, b):
    return lax.dot_general(a, b, (((0,), (0,)), ((), ())), preferred_element_type=jnp.float32)


def _sigmoid(x):
    return 1.0 / (1.0 + jnp.exp(-x))


def _silu(x):
    return x / (1.0 + jnp.exp(-x))


def _log_sigmoid(x):
    return jnp.minimum(x, 0.0) - jnp.log1p(jnp.exp(-jnp.abs(x)))


def _split3(x):
    hi = x.astype(jnp.bfloat16)
    r1 = x - hi.astype(jnp.float32)
    mid = r1.astype(jnp.bfloat16)
    lo = (r1 - mid.astype(jnp.float32)).astype(jnp.bfloat16)
    return hi, mid, lo


def _proj_kernel(x_ref, xp_ref, xn_ref_, nw_ref, w_ref, wg_ref, wgt_ref, bg_ref, bgt_ref, cw_ref, cb_ref,
                 o_ref, g_ref, gt_ref, xn_s, acc_s):
    i = pl.program_id(1)
    j = pl.program_id(2)
    ni = pl.num_programs(1)
    tm = x_ref.shape[1]

    def norm(x):
        y = x * lax.rsqrt(jnp.mean(x * x, axis=-1, keepdims=True) + EPS)
        return y * nw_ref[...]

    @pl.when(j == 0)
    def _():
        cur = norm(x_ref[0]).astype(jnp.bfloat16)
        prev = jnp.where(i > 0, norm(xp_ref[0]), 0.0).astype(jnp.bfloat16)
        nxt = jnp.where(i < ni - 1, norm(xn_ref_[0]), 0.0).astype(jnp.bfloat16)
        xn_s[0:HALO, :] = prev
        xn_s[HALO:HALO + tm, :] = cur
        xn_s[HALO + tm:2 * HALO + tm, :] = nxt
        g_ref[0] = _dot(cur, wg_ref[...]) + bg_ref[...]
        gt_ref[0] = _dot_nt(wgt_ref[...], cur) + bgt_ref[...]

    @pl.when(j <= COL_MK)
    def _():
        acc_s[...] = _dot(xn_s[...], w_ref[...])
        pad = CONV_W // 2
        y = cb_ref[...]
        for t in range(CONV_W):
            y = y + acc_s[pl.ds(HALO - pad + t, tm), :] * cw_ref[t:t + 1, :]
        y = _silu(y)
        y = y * jnp.where(j == COL_MK, M_HEAD_DIM ** -0.5, 1.0)
        o_ref[0] = y.astype(o_ref.dtype)

    @pl.when(j > COL_MK)
    def _():
        o_ref[0] = _dot(xn_s[HALO:HALO + tm, :], w_ref[...]).astype(o_ref.dtype)


def _proj(x, norm_w, w_main, w_g, w_gt, b_g, b_gt, conv_w, conv_b):
    B, T, D = x.shape
    tm = min(PROJ_TM, T)
    assert T % tm == 0 and tm % HALO == 0
    ni = T // tm
    hb = tm // HALO
    grid = (B, ni, N_COL_TILES)
    return pl.pallas_call(
        _proj_kernel,
        grid=grid,
        in_specs=[
            pl.BlockSpec((1, tm, D), lambda b, i, j: (b, i, 0)),
            pl.BlockSpec((1, HALO, D), lambda b, i, j: (b, jnp.maximum(i * hb - 1, 0), 0)),
            pl.BlockSpec((1, HALO, D), lambda b, i, j: (b, jnp.minimum((i + 1) * hb, T // HALO - 1), 0)),
            pl.BlockSpec((1, D), lambda b, i, j: (0, 0)),
            pl.BlockSpec((D, D_MODEL), lambda b, i, j: (0, j)),
            pl.BlockSpec((D, LANES), lambda b, i, j: (0, 0)),
            pl.BlockSpec((N_GATES, D), lambda b, i, j: (0, 0)),
            pl.BlockSpec((1, LANES), lambda b, i, j: (0, 0)),
            pl.BlockSpec((N_GATES, 1), lambda b, i, j: (0, 0)),
            pl.BlockSpec((8, D_MODEL), lambda b, i, j: (0, jnp.minimum(j, COL_MK))),
            pl.BlockSpec((1, D_MODEL), lambda b, i, j: (0, jnp.minimum(j, COL_MK))),
        ],
        out_specs=[
            pl.BlockSpec((1, tm, D_MODEL), lambda b, i, j: (b, i, j)),
            pl.BlockSpec((1, tm, LANES), lambda b, i, j: (b, i, 0)),
            pl.BlockSpec((1, N_GATES, tm), lambda b, i, j: (b, 0, i)),
        ],
        out_shape=[
            jax.ShapeDtypeStruct((B, T, D_MAIN), jnp.bfloat16),
            jax.ShapeDtypeStruct((B, T, LANES), jnp.float32),
            jax.ShapeDtypeStruct((B, N_GATES, T), jnp.float32),
        ],
        scratch_shapes=[
            pltpu.VMEM((tm + 2 * HALO, D), jnp.bfloat16),
            pltpu.VMEM((tm + 2 * HALO, D_MODEL), jnp.float32),
        ],
        compiler_params=pltpu.CompilerParams(
            dimension_semantics=("arbitrary", "arbitrary", "arbitrary"),
            vmem_limit_bytes=VMEM_LIMIT_BYTES),
        name="proj",
    )(x, x, x, norm_w, w_main, w_g, w_gt, b_g, b_gt, conv_w, conv_b)


def _mlstm_dir(d, q_ref, k_ref, v_ref, g_ref, gt_ref, h_ref, s_ref, n_ref, m_ref):
    L = q_ref.shape[1]
    row = lax.broadcasted_iota(jnp.int32, (L, L), 0)
    col = lax.broadcasted_iota(jnp.int32, (L, L), 1)
    if d == 0:
        mask = col <= row
    else:
        mask = col >= row
    ones_c = jnp.where(mask, 1.0, 0.0).astype(jnp.bfloat16)
    ones_r = jnp.where(mask, 1.0, 0.0).astype(jnp.bfloat16)

    g = g_ref[0]
    gt = gt_ref[0]
    lf_c = _log_sigmoid(g)
    lf_r = _log_sigmoid(gt)
    b_c = sum(_dot(ones_c, p) for p in _split3(lf_c))
    b_r = sum(_dot_nt(p, ones_r) for p in _split3(lf_r))

    last = L - 1 if d == 0 else 0
    for h in range(M_HEADS):
        gi = d * 2 * M_HEADS + h
        gf = gi + M_HEADS
        idx = d * M_HEADS + h
        lanes = slice(h * M_HEAD_DIM, (h + 1) * M_HEAD_DIM)
        q = q_ref[0, :, lanes]
        k = k_ref[0, :, lanes]
        v = v_ref[0, :, lanes]
        b_col = b_c[:, gf:gf + 1]
        li_col = g[:, gi:gi + 1]
        a_row = gt[gi:gi + 1, :] - b_r[gf:gf + 1, :]
        b_last = b_c[last:last + 1, gf:gf + 1]
        m_old = m_ref[idx]
        s_old = s_ref[idx]
        n_old = n_ref[idx]

        dm = jnp.where(mask, b_col + a_row, -jnp.inf)
        inter = b_col + m_old
        mj = jnp.maximum(inter, jnp.max(dm, axis=1, keepdims=True))
        p = _dot_nt(q, k) * jnp.exp(dm - mj)
        w_int = jnp.exp(inter - mj)
        qf = q.astype(jnp.float32)
        kf = k.astype(jnp.float32)
        num = w_int * _dot(q, s_old.astype(jnp.bfloat16)) + _dot(p.astype(jnp.bfloat16), v)
        den = w_int * jnp.sum(qf * n_old, axis=1, keepdims=True) + jnp.sum(p, axis=1, keepdims=True)
        h_ref[0, :, lanes] = num / jnp.maximum(jnp.abs(den), jnp.exp(-mj))

        gg = b_last - b_col + li_col
        m_new = jnp.maximum(b_last + m_old, jnp.max(gg, axis=0, keepdims=True))
        decay = jnp.exp(b_last + m_old - m_new)
        kw = kf * jnp.exp(gg - m_new)
        s_ref[idx] = decay * s_old + _dot_tn(kw.astype(jnp.bfloat16), v)
        n_ref[idx] = decay * n_old + jnp.sum(kw, axis=0, keepdims=True)
        m_ref[idx] = m_new


def _mlstm_kernel(qf_ref, kf_ref, vf_ref, gf_ref, gtf_ref, qb_ref, kb_ref, vb_ref, gb_ref, gtb_ref,
                  hf_ref, hb_ref, s_ref, n_ref, m_ref):
    @pl.when(pl.program_id(1) == 0)
    def _():
        s_ref[...] = jnp.zeros_like(s_ref)
        n_ref[...] = jnp.zeros_like(n_ref)
        m_ref[...] = jnp.zeros_like(m_ref)

    _mlstm_dir(0, qf_ref, kf_ref, vf_ref, gf_ref, gtf_ref, hf_ref, s_ref, n_ref, m_ref)
    _mlstm_dir(1, qb_ref, kb_ref, vb_ref, gb_ref, gtb_ref, hb_ref, s_ref, n_ref, m_ref)


def _mlstm(proj, g_tm, g_t):
    B, T, _ = proj.shape
    L = min(MLSTM_L, T)
    assert T % L == 0
    nc = T // L

    def fwd(col):
        return pl.BlockSpec((1, L, M_WIDTH), lambda b, c, col=col: (b, c, col))

    def bwd(col):
        return pl.BlockSpec((1, L, M_WIDTH), lambda b, c, col=col: (b, nc - 1 - c, col))

    return pl.pallas_call(
        _mlstm_kernel,
        grid=(B, nc),
        in_specs=[
            fwd(COL_MQ), fwd(COL_MK), fwd(COL_MV),
            pl.BlockSpec((1, L, LANES), lambda b, c: (b, c, 0)),
            pl.BlockSpec((1, N_GATES, L), lambda b, c: (b, 0, c)),
            bwd(COL_MQ), bwd(COL_MK), bwd(COL_MV),
            pl.BlockSpec((1, L, LANES), lambda b, c: (b, nc - 1 - c, 0)),
            pl.BlockSpec((1, N_GATES, L), lambda b, c: (b, 0, nc - 1 - c)),
        ],
        out_specs=[
            pl.BlockSpec((1, L, M_WIDTH), lambda b, c: (b, c, 0)),
            pl.BlockSpec((1, L, M_WIDTH), lambda b, c: (b, nc - 1 - c, 0)),
        ],
        out_shape=[
            jax.ShapeDtypeStruct((B, T, M_WIDTH), jnp.float32),
            jax.ShapeDtypeStruct((B, T, M_WIDTH), jnp.float32),
        ],
        scratch_shapes=[
            pltpu.VMEM((2 * M_HEADS, M_HEAD_DIM, M_HEAD_DIM), jnp.float32),
            pltpu.VMEM((2 * M_HEADS, 1, M_HEAD_DIM), jnp.float32),
            pltpu.VMEM((2 * M_HEADS, 1, 1), jnp.float32),
        ],
        compiler_params=pltpu.CompilerParams(
            dimension_semantics=("arbitrary", "arbitrary"),
            vmem_limit_bytes=VMEM_LIMIT_BYTES),
        name="mlstm",
    )(proj, proj, proj, g_tm, g_t, proj, proj, proj, g_tm, g_t)


_NA_KB = tuple(int(v) for v in np.clip(np.arange(GRID_W // NA_QCB) * NA_QCB - NA_KW // 2, 0, GRID_W - NA_KCB))
NA_NCB = GRID_W // NA_QCB
NA_MQ = NA_RB * NA_QCB
NA_NK = NA_WIN * NA_KCB


def _na_bias_table(rpb):
    i = np.arange(NA_RB)[:, None, None, None]
    qc = np.arange(NA_QCB)[None, :, None, None]
    jr = np.arange(NA_WIN)[None, None, :, None]
    kc = np.arange(NA_KCB)[None, None, None, :]
    dr_l, dc_l, ok_l = [], [], []
    for rt in range(3):
        off = (0, -NA_RB // 2, -NA_RB)[rt]
        if rt == 0:
            rs = np.maximum(i - NA_MAX_KH // 2, 0)
        elif rt == 1:
            rs = i - NA_MAX_KH // 2
        else:
            rs = np.minimum(i - NA_MAX_KH // 2, 0)
        krow = off + jr
        row_ok = (krow >= rs) & (krow < rs + NA_MAX_KH)
        dr = krow - i
        for n in range(NA_NCB):
            qcol = n * NA_QCB + qc
            kcol = _NA_KB[n] + kc
            cs = np.clip(qcol - NA_KW // 2, 0, GRID_W - NA_KW)
            col_ok = (kcol >= cs) & (kcol < cs + NA_KW)
            dc = kcol - qcol
            shape = (NA_RB, NA_QCB, NA_WIN, NA_KCB)
            dr_l.append(np.broadcast_to(np.clip(dr + NA_MAX_KH - 1, 0, 2 * NA_MAX_KH - 2), shape))
            dc_l.append(np.broadcast_to(np.clip(dc + NA_KW - 1, 0, 2 * NA_KW - 2), shape))
            ok_l.append(np.broadcast_to(row_ok & col_ok, shape))
    dr_i = np.stack(dr_l).reshape(3 * NA_NCB, NA_MQ, NA_NK)
    dc_i = np.stack(dc_l).reshape(3 * NA_NCB, NA_MQ, NA_NK)
    ok = np.stack(ok_l).reshape(3 * NA_NCB, NA_MQ, NA_NK)
    tab = jnp.where(ok[None], rpb.astype(jnp.float32)[:, dr_i, dc_i], NEG)
    tab = tab.reshape(NA_HEADS // NA_HPB, NA_HPB, 3 * NA_NCB, NA_MQ, NA_NK)
    return tab.transpose(0, 2, 1, 3, 4).reshape(NA_HEADS // NA_HPB, 3 * NA_NCB, NA_HPB * NA_MQ, NA_NK)


def _natten_kernel(q_ref, k0, k1, k2, k3, v0, v1, v2, v3, z_ref, tab_ref, o_ref, acc_s):
    rb = pl.program_id(2)
    nrb = pl.num_programs(2)
    rt = jnp.where(rb == 0, 0, jnp.where(rb == nrb - 1, 2, 1))
    piece = NA_WIN // 4 * GRID_W
    kf = jnp.concatenate([r[0].astype(jnp.float32) for r in (k0, k1, k2, k3)], axis=0)
    vf = jnp.concatenate([r[0].astype(jnp.float32) for r in (v0, v1, v2, v3)], axis=0)
    del piece
    q = q_ref[0]
    lane = lax.broadcasted_iota(jnp.int32, (NA_MQ, NA_LANES), 1) // NA_HEAD_DIM
    for n in range(NA_NCB):
        kb = _NA_KB[n]
        kn = jnp.concatenate([kf[j * GRID_W + kb:j * GRID_W + kb + NA_KCB] for j in range(NA_WIN)],
                             axis=0).astype(jnp.bfloat16)
        vn = jnp.concatenate([vf[j * GRID_W + kb:j * GRID_W + kb + NA_KCB] for j in range(NA_WIN)],
                             axis=0).astype(jnp.bfloat16)
        qn = jnp.concatenate([q[i * GRID_W + n * NA_QCB:i * GRID_W + (n + 1) * NA_QCB] for i in range(NA_RB)],
                             axis=0)
        qn = qn * (NA_HEAD_DIM ** -0.5)
        qs = jnp.concatenate([jnp.where(lane == hh, qn, 0.0).astype(jnp.bfloat16) for hh in range(NA_HPB)],
                             axis=0)
        s = _dot_nt(qs, kn) + tab_ref[0, rt * NA_NCB + n]
        m = jnp.max(s, axis=1, keepdims=True)
        e = jnp.exp(s - m)
        l = jnp.sum(e, axis=1, keepdims=True)
        o = _dot(e.astype(jnp.bfloat16), vn) / l
        on = jnp.zeros((NA_MQ, NA_LANES), jnp.float32)
        for hh in range(NA_HPB):
            on = jnp.where(lane == hh, o[hh * NA_MQ:(hh + 1) * NA_MQ], on)
        for i in range(NA_RB):
            acc_s[i * GRID_W + n * NA_QCB:i * GRID_W + (n + 1) * NA_QCB, :] = on[i * NA_QCB:(i + 1) * NA_QCB]
    o_ref[0] = (acc_s[...] * _silu(z_ref[0].astype(jnp.float32))).astype(o_ref.dtype)


def _natten(proj, table):
    B, T, _ = proj.shape
    rows = T // GRID_W
    assert T % GRID_W == 0 and rows % NA_RB == 0 and rows >= NA_WIN
    nrb = rows // NA_RB
    ng = NA_HEADS // NA_HPB
    tq = NA_RB * GRID_W
    pr = NA_WIN // 4
    tp = pr * GRID_W
    cpt = D_MODEL // NA_LANES

    def win(col, p):
        def index(g, b, rb):
            start = jnp.clip(rb * (NA_RB // pr) - NA_RB // 2 // pr, 0, rows // pr - NA_WIN // pr)
            return (b, start + p, col * cpt + g)
        return pl.BlockSpec((1, tp, NA_LANES), index)

    return pl.pallas_call(
        _natten_kernel,
        grid=(ng, B, nrb),
        in_specs=[pl.BlockSpec((1, tq, NA_LANES), lambda g, b, rb: (b, rb, COL_NQ * cpt + g))]
        + [win(COL_NK, p) for p in range(4)]
        + [win(COL_NV, p) for p in range(4)]
        + [pl.BlockSpec((1, tq, NA_LANES), lambda g, b, rb: (b, rb, COL_NZ * cpt + g)),
           pl.BlockSpec((1, 3 * NA_NCB, NA_HPB * NA_MQ, NA_NK), lambda g, b, rb: (g, 0, 0, 0))],
        out_specs=pl.BlockSpec((1, tq, NA_LANES), lambda g, b, rb: (b, rb, g)),
        out_shape=jax.ShapeDtypeStruct((B, T, NA_WIDTH), jnp.bfloat16),
        scratch_shapes=[pltpu.VMEM((tq, NA_LANES), jnp.float32)],
        compiler_params=pltpu.CompilerParams(
            dimension_semantics=("arbitrary", "arbitrary", "arbitrary"),
            vmem_limit_bytes=VMEM_LIMIT_BYTES),
        name="natten",
    )(proj, *([proj] * 8), proj, table)


def _out_kernel(x_ref, hf_ref, hb_ref, o_ref, z_ref, yb_ref, ga_ref, gb_ref, mhw_ref, wda_ref, wdb_ref, wo_ref,
                fnw_ref, y_ref):
    f32 = jnp.float32
    h = (hf_ref[0] + hb_ref[0]) * _sigmoid(o_ref[0].astype(f32))
    parts = []
    for hd in range(M_HEADS):
        hh = h[:, hd * M_HEAD_DIM:(hd + 1) * M_HEAD_DIM]
        parts.append(hh * lax.rsqrt(jnp.mean(hh * hh, axis=-1, keepdims=True) + EPS))
    hn = jnp.concatenate(parts, axis=-1) * mhw_ref[...]
    ya = (hn * _silu(z_ref[0].astype(f32))).astype(jnp.bfloat16)
    merged = (_sigmoid(ga_ref[0].astype(f32)) * _dot(ya, wda_ref[...])
              + _sigmoid(gb_ref[0].astype(f32)) * _dot(yb_ref[0], wdb_ref[...]))
    y = x_ref[0] + _dot(merged.astype(jnp.bfloat16), wo_ref[...])
    y = y * lax.rsqrt(jnp.mean(y * y, axis=-1, keepdims=True) + EPS)
    y_ref[0] = y * fnw_ref[...]


def _out(x, h_f, h_b, proj, y_b, mh_w, w_da, w_db, w_o, fn_w):
    B, T, D = x.shape
    tm = min(OUT_TM, T)
    assert T % tm == 0

    def tok(col=0):
        return pl.BlockSpec((1, tm, D_MODEL), lambda b, i, col=col: (b, i, col))

    def const(shape):
        return pl.BlockSpec(shape, lambda b, i: (0,) * len(shape))

    return pl.pallas_call(
        _out_kernel,
        grid=(B, T // tm),
        in_specs=[tok(), tok(), tok(), tok(COL_MO), tok(COL_MZ), tok(), tok(COL_GA), tok(COL_GB),
                  const((1, D)), const((D, D)), const((D, D)), const((D, D)), const((1, D))],
        out_specs=tok(),
        out_shape=jax.ShapeDtypeStruct((B, T, D), jnp.float32),
        compiler_params=pltpu.CompilerParams(
            dimension_semantics=("arbitrary", "arbitrary"),
            vmem_limit_bytes=VMEM_LIMIT_BYTES),
        name="out",
    )(x, h_f, h_b, proj, proj, y_b, proj, proj, mh_w, w_da, w_db, w_o, fn_w)


def _prepare(norm_w, w_in, b_gate, conv_w, conv_b, mh_norm_w, rpb, w_down_a, w_down_b, w_out, final_norm_w):
    bf16 = jnp.bfloat16
    g0 = 5 * M_WIDTH
    w_main = jnp.concatenate([w_in[:, :g0], w_in[:, g0 + N_GATES:]], axis=1).astype(bf16)
    w_gate = w_in[:, g0:g0 + N_GATES]
    w_g = jnp.pad(w_gate, ((0, 0), (0, LANES - N_GATES))).astype(bf16)
    w_gt = w_gate.T.astype(bf16)
    b_g = jnp.pad(b_gate, (0, LANES - N_GATES)).reshape(1, LANES)
    b_gt = b_gate.reshape(N_GATES, 1)
    cw = jnp.pad(conv_w, ((0, 8 - CONV_W), (0, 0)))
    return dict(
        norm_w=norm_w.reshape(1, D_MODEL), w_main=w_main, w_g=w_g, w_gt=w_gt, b_g=b_g, b_gt=b_gt,
        conv_w=cw, conv_b=conv_b.reshape(1, 2 * M_WIDTH), mh_w=mh_norm_w.reshape(1, M_WIDTH),
        table=_na_bias_table(rpb), w_da=w_down_a.astype(bf16), w_db=w_down_b.astype(bf16),
        w_o=w_out.astype(bf16), fn_w=final_norm_w.reshape(1, D_MODEL))


def _trunk(x, p):
    proj, g_tm, g_t = _proj(x, p["norm_w"], p["w_main"], p["w_g"], p["w_gt"], p["b_g"], p["b_gt"],
                            p["conv_w"], p["conv_b"])
    h_f, h_b = _mlstm(proj, g_tm, g_t)
    y_b = _natten(proj, p["table"])
    return _out(x, h_f, h_b, proj, y_b, p["mh_w"], p["w_da"], p["w_db"], p["w_o"], p["fn_w"])


def kernel(x_prompt, x_sample, norm_w, w_in, b_gate, conv_w, conv_b, mh_norm_w, rpb, w_down_a, w_down_b, w_out,
           final_norm_w):
    assert norm_w.shape[0] == 1, "single-layer trunk"
    p = _prepare(norm_w[0], w_in[0], b_gate[0], conv_w[0], conv_b[0], mh_norm_w[0], rpb[0],
                 w_down_a[0], w_down_b[0], w_out[0], final_norm_w)
    return (_trunk(x_prompt, p), _trunk(x_sample, p))
```

```python
import functools

import numpy as np
import jax
import jax.numpy as jnp
from jax import lax
from jax.experimental import pallas as pl
from jax.experimental.pallas import tpu as pltpu

D_MODEL = 1024
GRID_W = 64
M_HEADS = 4
M_HEAD_DIM = 256
M_WIDTH = M_HEADS * M_HEAD_DIM
CONV_W = 5
N_GATES = 4 * M_HEADS
NA_HEADS = 16
NA_HEAD_DIM = 64
NA_WIDTH = NA_HEADS * NA_HEAD_DIM
NA_MAX_KH = 8
NA_KW = 16
NA_QCB = 16
NA_KCB = 32
EPS = 1e-6
NEG = -1e30

LANES = 128
BF16_SUBLANES = 16
VMEM_LIMIT_BYTES = 56 * 1024 * 1024

COL_MQ, COL_MK, COL_MV, COL_MO, COL_MZ, COL_NQ, COL_NK, COL_NV, COL_NZ, COL_GA, COL_GB = range(11)
N_COL_TILES = 11
D_MAIN = N_COL_TILES * D_MODEL

PROJ_TM = 1024
HALO = BF16_SUBLANES
MLSTM_L = 256
NA_RB = 8
NA_WIN = 16
NA_HPB = 4
NA_LANES = NA_HPB * NA_HEAD_DIM
OUT_TM = 512


def _dot(a, b):
    return jnp.dot(a, b, preferred_element_type=jnp.float32)


def _dot_nt(a, b):
    return lax.dot_general(a, b, (((1,), (1,)), ((), ())), preferred_element_type=jnp.float32)


def _dot_tn(a, b):
    return lax.dot_general(a, b, (((0,), (0,)), ((), ())), preferred_element_type=jnp.float32)


def _sigmoid(x):
    return 1.0 / (1.0 + jnp.exp(-x))


def _silu(x):
    return x / (1.0 + jnp.exp(-x))


def _log_sigmoid(x):
    return jnp.minimum(x, 0.0) - jnp.log1p(jnp.exp(-jnp.abs(x)))


def _split3(x):
    hi = x.astype(jnp.bfloat16)
    r1 = x - hi.astype(jnp.float32)
    mid = r1.astype(jnp.bfloat16)
    lo = (r1 - mid.astype(jnp.float32)).astype(jnp.bfloat16)
    return hi, mid, lo


def _proj_kernel(x_ref, xp_ref, xn_ref_, nw_ref, w_ref, wg_ref, wgt_ref, bg_ref, bgt_ref, cw_ref, cb_ref,
                 o_ref, g_ref, gt_ref, xn_s, acc_s):
    i = pl.program_id(1)
    j = pl.program_id(2)
    ni = pl.num_programs(1)
    tm = x_ref.shape[1]

    def norm(x):
        y = x * lax.rsqrt(jnp.mean(x * x, axis=-1, keepdims=True) + EPS)
        return y * nw_ref[...]

    @pl.when(j == 0)
    def _():
        cur = norm(x_ref[0]).astype(jnp.bfloat16)
        prev = jnp.where(i > 0, norm(xp_ref[0]), 0.0).astype(jnp.bfloat16)
        nxt = jnp.where(i < ni - 1, norm(xn_ref_[0]), 0.0).astype(jnp.bfloat16)
        xn_s[0:HALO, :] = prev
        xn_s[HALO:HALO + tm, :] = cur
        xn_s[HALO + tm:2 * HALO + tm, :] = nxt
        g_ref[0] = _dot(cur, wg_ref[...]) + bg_ref[...]
        gt_ref[0] = _dot_nt(wgt_ref[...], cur) + bgt_ref[...]

    @pl.when(j <= COL_MK)
    def _():
        acc_s[...] = _dot(xn_s[...], w_ref[...])
        pad = CONV_W // 2
        y = cb_ref[...]
        for t in range(CONV_W):
            y = y + acc_s[pl.ds(HALO - pad + t, tm), :] * cw_ref[t:t + 1, :]
        y = _silu(y)
        y = y * jnp.where(j == COL_MK, M_HEAD_DIM ** -0.5, 1.0)
        o_ref[0] = y.astype(o_ref.dtype)

    @pl.when(j > COL_MK)
    def _():
        o_ref[0] = _dot(xn_s[HALO:HALO + tm, :], w_ref[...]).astype(o_ref.dtype)


def _proj(x, norm_w, w_main, w_g, w_gt, b_g, b_gt, conv_w, conv_b):
    B, T, D = x.shape
    tm = min(PROJ_TM, T)
    assert T % tm == 0 and tm % HALO == 0
    ni = T // tm
    hb = tm // HALO
    grid = (B, ni, N_COL_TILES)
    return pl.pallas_call(
        _proj_kernel,
        grid=grid,
        in_specs=[
            pl.BlockSpec((1, tm, D), lambda b, i, j: (b, i, 0)),
            pl.BlockSpec((1, HALO, D), lambda b, i, j: (b, jnp.maximum(i * hb - 1, 0), 0)),
            pl.BlockSpec((1, HALO, D), lambda b, i, j: (b, jnp.minimum((i + 1) * hb, T // HALO - 1), 0)),
            pl.BlockSpec((1, D), lambda b, i, j: (0, 0)),
            pl.BlockSpec((D, D_MODEL), lambda b, i, j: (0, j)),
            pl.BlockSpec((D, LANES), lambda b, i, j: (0, 0)),
            pl.BlockSpec((N_GATES, D), lambda b, i, j: (0, 0)),
            pl.BlockSpec((1, LANES), lambda b, i, j: (0, 0)),
            pl.BlockSpec((N_GATES, 1), lambda b, i, j: (0, 0)),
            pl.BlockSpec((8, D_MODEL), lambda b, i, j: (0, jnp.minimum(j, COL_MK))),
            pl.BlockSpec((1, D_MODEL), lambda b, i, j: (0, jnp.minimum(j, COL_MK))),
        ],
        out_specs=[
            pl.BlockSpec((1, tm, D_MODEL), lambda b, i, j: (b, i, j)),
            pl.BlockSpec((1, tm, LANES), lambda b, i, j: (b, i, 0)),
            pl.BlockSpec((1, N_GATES, tm), lambda b, i, j: (b, 0, i)),
        ],
        out_shape=[
            jax.ShapeDtypeStruct((B, T, D_MAIN), jnp.bfloat16),
            jax.ShapeDtypeStruct((B, T, LANES), jnp.float32),
            jax.ShapeDtypeStruct((B, N_GATES, T), jnp.float32),
        ],
        scratch_shapes=[
            pltpu.VMEM((tm + 2 * HALO, D), jnp.bfloat16),
            pltpu.VMEM((tm + 2 * HALO, D_MODEL), jnp.float32),
        ],
        compiler_params=pltpu.CompilerParams(
            dimension_semantics=("arbitrary", "arbitrary", "arbitrary"),
            vmem_limit_bytes=VMEM_LIMIT_BYTES),
        name="proj",
    )(x, x, x, norm_w, w_main, w_g, w_gt, b_g, b_gt, conv_w, conv_b)


def _mlstm_dir(d, q_ref, k_ref, v_ref, g_ref, gt_ref, h_ref, s_ref, n_ref, m_ref):
    L = q_ref.shape[1]
    row = lax.broadcasted_iota(jnp.int32, (L, L), 0)
    col = lax.broadcasted_iota(jnp.int32, (L, L), 1)
    if d == 0:
        mask = col <= row
    else:
        mask = col >= row
    ones_c = jnp.where(mask, 1.0, 0.0).astype(jnp.bfloat16)
    ones_r = jnp.where(mask, 1.0, 0.0).astype(jnp.bfloat16)

    g = g_ref[0]
    gt = gt_ref[0]
    lf_c = _log_sigmoid(g)
    lf_r = _log_sigmoid(gt)
    b_c = sum(_dot(ones_c, p) for p in _split3(lf_c))
    b_r = sum(_dot_nt(p, ones_r) for p in _split3(lf_r))

    last = L - 1 if d == 0 else 0
    for h in range(M_HEADS):
        gi = d * 2 * M_HEADS + h
        gf = gi + M_HEADS
        idx = d * M_HEADS + h
        lanes = slice(h * M_HEAD_DIM, (h + 1) * M_HEAD_DIM)
        q = q_ref[0, :, lanes]
        k = k_ref[0, :, lanes]
        v = v_ref[0, :, lanes]
        b_col = b_c[:, gf:gf + 1]
        li_col = g[:, gi:gi + 1]
        a_row = gt[gi:gi + 1, :] - b_r[gf:gf + 1, :]
        b_last = b_c[last:last + 1, gf:gf + 1]
        m_old = m_ref[idx]
        s_old = s_ref[idx]
        n_old = n_ref[idx]

        dm = jnp.where(mask, b_col + a_row, -jnp.inf)
        inter = b_col + m_old
        mj = jnp.maximum(inter, jnp.max(dm, axis=1, keepdims=True))
        p = _dot_nt(q, k) * jnp.exp(dm - mj)
        w_int = jnp.exp(inter - mj)
        qf = q.astype(jnp.float32)
        kf = k.astype(jnp.float32)
        num = w_int * _dot(q, s_old.astype(jnp.bfloat16)) + _dot(p.astype(jnp.bfloat16), v)
        den = w_int * jnp.sum(qf * n_old, axis=1, keepdims=True) + jnp.sum(p, axis=1, keepdims=True)
        h_ref[0, :, lanes] = num / jnp.maximum(jnp.abs(den), jnp.exp(-mj))

        gg = b_last - b_col + li_col
        m_new = jnp.maximum(b_last + m_old, jnp.max(gg, axis=0, keepdims=True))
        decay = jnp.exp(b_last + m_old - m_new)
        kw = kf * jnp.exp(gg - m_new)
        s_ref[idx] = decay * s_old + _dot_tn(kw.astype(jnp.bfloat16), v)
        n_ref[idx] = decay * n_old + jnp.sum(kw, axis=0, keepdims=True)
        m_ref[idx] = m_new


def _mlstm_kernel(qf_ref, kf_ref, vf_ref, gf_ref, gtf_ref, qb_ref, kb_ref, vb_ref, gb_ref, gtb_ref,
                  hf_ref, hb_ref, s_ref, n_ref, m_ref):
    @pl.when(pl.program_id(1) == 0)
    def _():
        s_ref[...] = jnp.zeros_like(s_ref)
        n_ref[...] = jnp.zeros_like(n_ref)
        m_ref[...] = jnp.zeros_like(m_ref)

    _mlstm_dir(0, qf_ref, kf_ref, vf_ref, gf_ref, gtf_ref, hf_ref, s_ref, n_ref, m_ref)
    _mlstm_dir(1, qb_ref, kb_ref, vb_ref, gb_ref, gtb_ref, hb_ref, s_ref, n_ref, m_ref)


def _mlstm(proj, g_tm, g_t):
    B, T, _ = proj.shape
    L = min(MLSTM_L, T)
    assert T % L == 0
    nc = T // L

    def fwd(col):
        return pl.BlockSpec((1, L, M_WIDTH), lambda b, c, col=col: (b, c, col))

    def bwd(col):
        return pl.BlockSpec((1, L, M_WIDTH), lambda b, c, col=col: (b, nc - 1 - c, col))

    return pl.pallas_call(
        _mlstm_kernel,
        grid=(B, nc),
        in_specs=[
            fwd(COL_MQ), fwd(COL_MK), fwd(COL_MV),
            pl.BlockSpec((1, L, LANES), lambda b, c: (b, c, 0)),
            pl.BlockSpec((1, N_GATES, L), lambda b, c: (b, 0, c)),
            bwd(COL_MQ), bwd(COL_MK), bwd(COL_MV),
            pl.BlockSpec((1, L, LANES), lambda b, c: (b, nc - 1 - c, 0)),
            pl.BlockSpec((1, N_GATES, L), lambda b, c: (b, 0, nc - 1 - c)),
        ],
        out_specs=[
            pl.BlockSpec((1, L, M_WIDTH), lambda b, c: (b, c, 0)),
            pl.BlockSpec((1, L, M_WIDTH), lambda b, c: (b, nc - 1 - c, 0)),
        ],
        out_shape=[
            jax.ShapeDtypeStruct((B, T, M_WIDTH), jnp.float32),
            jax.ShapeDtypeStruct((B, T, M_WIDTH), jnp.float32),
        ],
        scratch_shapes=[
            pltpu.VMEM((2 * M_HEADS, M_HEAD_DIM, M_HEAD_DIM), jnp.float32),
            pltpu.VMEM((2 * M_HEADS, 1, M_HEAD_DIM), jnp.float32),
            pltpu.VMEM((2 * M_HEADS, 1, 1), jnp.float32),
        ],
        compiler_params=pltpu.CompilerParams(
            dimension_semantics=("arbitrary", "arbitrary"),
            vmem_limit_bytes=VMEM_LIMIT_BYTES),
        name="mlstm",
    )(proj, proj, proj, g_tm, g_t, proj, proj, proj, g_tm, g_t)


_NA_KB = tuple(int(v) for v in np.clip(np.arange(GRID_W // NA_QCB) * NA_QCB - NA_KW // 2, 0, GRID_W - NA_KCB))
NA_NCB = GRID_W // NA_QCB
NA_MQ = NA_RB * NA_QCB
NA_NK = NA_WIN * NA_KCB


def _na_bias_table(rpb):
    n_dr, n_dc = 2 * NA_MAX_KH - 1, 2 * NA_KW - 1
    i = np.arange(NA_RB)[:, None]
    jr = np.arange(NA_WIN)[None, :]
    row_sel = np.zeros((3, NA_RB, NA_WIN, n_dr), np.float32)
    for rt in range(3):
        off = (0, -NA_RB // 2, -NA_RB)[rt]
        if rt == 0:
            rs = np.maximum(i - NA_MAX_KH // 2, 0)
        elif rt == 1:
            rs = i - NA_MAX_KH // 2
        else:
            rs = np.minimum(i - NA_MAX_KH // 2, 0)
        krow = off + jr
        ii, jj = np.nonzero((krow >= rs) & (krow < rs + NA_MAX_KH))
        row_sel[rt, ii, jj, (krow - i)[ii, jj] + NA_MAX_KH - 1] = 1.0
    qc = np.arange(NA_QCB)[:, None]
    kc = np.arange(NA_KCB)[None, :]
    col_sel = np.zeros((NA_NCB, NA_QCB, NA_KCB, n_dc), np.float32)
    for n in range(NA_NCB):
        qcol = n * NA_QCB + qc
        kcol = _NA_KB[n] + kc
        cs = np.clip(qcol - NA_KW // 2, 0, GRID_W - NA_KW)
        qq, kk = np.nonzero((kcol >= cs) & (kcol < cs + NA_KW))
        col_sel[n, qq, kk, (kcol - qcol)[qq, kk] + NA_KW - 1] = 1.0
    ok = (np.einsum("xijr->xij", row_sel)[:, None, :, None, :, None]
          * np.einsum("nqkc->nqk", col_sel)[None, :, None, :, None, :]) > 0
    hi = lax.Precision.HIGHEST
    r = rpb.astype(jnp.float32).reshape(NA_HEADS // NA_HPB, NA_HPB, n_dr, n_dc)
    a = jnp.einsum("ghrc,xijr->gxhijc", r, row_sel, precision=hi)
    t = jnp.einsum("gxhijc,nqkc->gxnhiqjk", a, col_sel, precision=hi)
    t = jnp.where(ok[None, :, :, None], t, NEG)
    return t.reshape(NA_HEADS // NA_HPB, 3 * NA_NCB, NA_HPB * NA_MQ, NA_NK)


def _natten_kernel(q_ref, k0, k1, k2, k3, v0, v1, v2, v3, z_ref, tab_ref, o_ref, acc_s):
    rb = pl.program_id(2)
    nrb = pl.num_programs(2)
    rt = jnp.where(rb == 0, 0, jnp.where(rb == nrb - 1, 2, 1))
    kf =jnp.concatenate([r[0].astype(jnp.float32) for r in (k0, k1, k2, k3)], axis=0)
    vf = jnp.concatenate([r[0].astype(jnp.float32) for r in (v0, v1, v2, v3)], axis=0)
    q = q_ref[0]
    lane = lax.broadcasted_iota(jnp.int32, (NA_MQ, NA_LANES), 1) // NA_HEAD_DIM
    for n in range(NA_NCB):
        kb = _NA_KB[n]
        kn = jnp.concatenate([kf[j * GRID_W + kb:j * GRID_W + kb + NA_KCB] for j in range(NA_WIN)],
                             axis=0).astype(jnp.bfloat16)
        vn = jnp.concatenate([vf[j * GRID_W + kb:j * GRID_W + kb + NA_KCB] for j in range(NA_WIN)],
                             axis=0).astype(jnp.bfloat16)
        qn = jnp.concatenate([q[i * GRID_W + n * NA_QCB:i * GRID_W + (n + 1) * NA_QCB] for i in range(NA_RB)],
                             axis=0)
        qn = qn * (NA_HEAD_DIM ** -0.5)
        qs = jnp.concatenate([jnp.where(lane == hh, qn, 0.0).astype(jnp.bfloat16) for hh in range(NA_HPB)],
                             axis=0)
        s = _dot_nt(qs, kn) + tab_ref[0, rt * NA_NCB + n]
        m = jnp.max(s, axis=1, keepdims=True)
        e = jnp.exp(s - m)
        l = jnp.sum(e, axis=1, keepdims=True)
        o = _dot(e.astype(jnp.bfloat16), vn) / l
        on = jnp.zeros((NA_MQ, NA_LANES), jnp.float32)
        for hh in range(NA_HPB):
            on = jnp.where(lane == hh, o[hh * NA_MQ:(hh + 1) * NA_MQ], on)
        for i in range(NA_RB):
            acc_s[i * GRID_W + n * NA_QCB:i * GRID_W + (n + 1) * NA_QCB, :] = on[i * NA_QCB:(i + 1) * NA_QCB]
    o_ref[0] = (acc_s[...] * _silu(z_ref[0].astype(jnp.float32))).astype(o_ref.dtype)


def _natten(proj, table):
    B, T, _ = proj.shape
    rows = T // GRID_W
    assert T % GRID_W == 0 and rows % NA_RB == 0 and rows >= NA_WIN
    nrb = rows // NA_RB
    ng = NA_HEADS // NA_HPB
    tq = NA_RB * GRID_W
    pr = NA_WIN // 4
    tp = pr * GRID_W
    cpt = D_MODEL // NA_LANES

    def win(col, p):
        def index(g, b, rb):
            start = jnp.clip(rb * (NA_RB // pr) - NA_RB // 2 // pr, 0, rows // pr - NA_WIN // pr)
            return (b, start + p, col * cpt + g)
        return pl.BlockSpec((1, tp, NA_LANES), index)

    return pl.pallas_call(
        _natten_kernel,
        grid=(ng, B, nrb),
        in_specs=[pl.BlockSpec((1, tq, NA_LANES), lambda g, b, rb: (b, rb, COL_NQ * cpt + g))]
        + [win(COL_NK, p) for p in range(4)]
        + [win(COL_NV, p) for p in range(4)]
        + [pl.BlockSpec((1, tq, NA_LANES), lambda g, b, rb: (b, rb, COL_NZ * cpt + g)),
           pl.BlockSpec((1, 3 * NA_NCB, NA_HPB * NA_MQ, NA_NK), lambda g, b, rb: (g, 0, 0, 0))],
        out_specs=pl.BlockSpec((1, tq, NA_LANES), lambda g, b, rb: (b, rb, g)),
        out_shape=jax.ShapeDtypeStruct((B, T, NA_WIDTH), jnp.bfloat16),
        scratch_shapes=[pltpu.VMEM((tq, NA_LANES), jnp.float32)],
        compiler_params=pltpu.CompilerParams(
            dimension_semantics=("arbitrary", "arbitrary", "arbitrary"),
            vmem_limit_bytes=VMEM_LIMIT_BYTES),
        name="natten",
    )(proj, *([proj] * 8), proj, table)


def _out_kernel(x_ref, hf_ref, hb_ref, o_ref, z_ref, yb_ref, ga_ref, gb_ref, mhw_ref, wda_ref, wdb_ref, wo_ref,
                fnw_ref, y_ref):
    f32 = jnp.float32
    h = (hf_ref[0] + hb_ref[0]) * _sigmoid(o_ref[0].astype(f32))
    parts = []
    for hd in range(M_HEADS):
        hh = h[:, hd * M_HEAD_DIM:(hd + 1) * M_HEAD_DIM]
        parts.append(hh * lax.rsqrt(jnp.mean(hh * hh, axis=-1, keepdims=True) + EPS))
    hn = jnp.concatenate(parts, axis=-1) * mhw_ref[...]
    ya = (hn * _silu(z_ref[0].astype(f32))).astype(jnp.bfloat16)
    merged = (_sigmoid(ga_ref[0].astype(f32)) * _dot(ya, wda_ref[...])
              + _sigmoid(gb_ref[0].astype(f32)) * _dot(yb_ref[0], wdb_ref[...]))
    y = x_ref[0] + _dot(merged.astype(jnp.bfloat16), wo_ref[...])
    y = y * lax.rsqrt(jnp.mean(y * y, axis=-1, keepdims=True) + EPS)
    y_ref[0] = y * fnw_ref[...]


def _out(x, h_f, h_b, proj, y_b, mh_w, w_da, w_db, w_o, fn_w):
    B, T, D = x.shape
    tm = min(OUT_TM, T)
    assert T % tm == 0

    def tok(col=0):
        return pl.BlockSpec((1, tm, D_MODEL), lambda b, i, col=col: (b, i, col))

    def const(shape):
        return pl.BlockSpec(shape, lambda b, i: (0,) * len(shape))

    return pl.pallas_call(
        _out_kernel,
        grid=(B, T // tm),
        in_specs=[tok(), tok(), tok(), tok(COL_MO), tok(COL_MZ), tok(), tok(COL_GA), tok(COL_GB),
                  const((1, D)), const((D, D)), const((D, D)), const((D, D)), const((1, D))],
        out_specs=tok(),
        out_shape=jax.ShapeDtypeStruct((B, T, D), jnp.float32),
        compiler_params=pltpu.CompilerParams(
            dimension_semantics=("arbitrary", "arbitrary"),
            vmem_limit_bytes=VMEM_LIMIT_BYTES),
        name="out",
    )(x, h_f, h_b, proj, proj, y_b, proj, proj, mh_w, w_da, w_db, w_o, fn_w)


def _prepare(norm_w, w_in, b_gate, conv_w, conv_b, mh_norm_w, rpb, w_down_a, w_down_b, w_out, final_norm_w):
    bf16 = jnp.bfloat16
    g0 = 5 * M_WIDTH
    w_main = jnp.concatenate([w_in[:, :g0], w_in[:, g0 + N_GATES:]], axis=1).astype(bf16)
    w_gate = w_in[:, g0:g0 + N_GATES]
    w_g = jnp.pad(w_gate, ((0, 0), (0, LANES - N_GATES))).astype(bf16)
    w_gt = w_gate.T.astype(bf16)
    b_g = jnp.pad(b_gate, (0, LANES - N_GATES)).reshape(1, LANES)
    b_gt = b_gate.reshape(N_GATES, 1)
    cw = jnp.pad(conv_w, ((0, 8 - CONV_W), (0, 0)))
    return dict(
        norm_w=norm_w.reshape(1, D_MODEL), w_main=w_main, w_g=w_g, w_gt=w_gt, b_g=b_g, b_gt=b_gt,
        conv_w=cw, conv_b=conv_b.reshape(1, 2 * M_WIDTH), mh_w=mh_norm_w.reshape(1, M_WIDTH),
        table=_na_bias_table(rpb), w_da=w_down_a.astype(bf16), w_db=w_down_b.astype(bf16),
        w_o=w_out.astype(bf16), fn_w=final_norm_w.reshape(1, D_MODEL))


def _trunk(x, p):
    proj, g_tm, g_t = _proj(x, p["norm_w"], p["w_main"], p["w_g"], p["w_gt"], p["b_g"], p["b_gt"],
                            p["conv_w"], p["conv_b"])
    h_f, h_b = _mlstm(proj, g_tm, g_t)
    y_b = _natten(proj, p["table"])
    return _out(x, h_f, h_b, proj, y_b, p["mh_w"], p["w_da"], p["w_db"], p["w_o"], p["fn_w"])


def kernel(x_prompt, x_sample, norm_w, w_in, b_gate, conv_w, conv_b, mh_norm_w, rpb, w_down_a, w_down_b, w_out,
           final_norm_w):
    assert norm_w.shape[0] == 1, "single-layer trunk"
    p = _prepare(norm_w[0], w_in[0], b_gate[0], conv_w[0], conv_b[0], mh_norm_w[0], rpb[0],
                 w_down_a[0], w_down_b[0], w_out[0], final_norm_w)
    return (_trunk(x_prompt, p), _trunk(x_sample, p))
```

```python
import functools

import numpy as np
import jax
import jax.numpy as jnp
from jax import lax
from jax.experimental import pallas as pl
from jax.experimental.pallas import tpu as pltpu

D_MODEL = 1024
GRID_W = 64
M_HEADS = 4
M_HEAD_DIM = 256
M_WIDTH = M_HEADS * M_HEAD_DIM
CONV_W = 5
N_GATES = 4 * M_HEADS
NA_HEADS = 16
NA_HEAD_DIM = 64
NA_WIDTH = NA_HEADS * NA_HEAD_DIM
NA_MAX_KH = 8
NA_KW = 16
NA_QCB = 16
NA_KCB = 32
EPS = 1e-6
NEG = -1e30

LANES = 128
SUBLANES = 8
BF16_SUBLANES = 16
VMEM_LIMIT_BYTES = 56 * 1024 * 1024

STEP_MQ, STEP_MK, STEP_MV = 0, 1, 2
N_STEPS = 11
COL_MQ, COL_MK, COL_MO, COL_MZ, COL_NQ, COL_NK, COL_NV, COL_NZ, COL_GA, COL_GB = range(10)
D_MAIN = 10 * D_MODEL

PROJ_TM = 1024
HALO = BF16_SUBLANES
MLSTM_L = 256
GATES_TT = 2048
M_AUG = M_HEAD_DIM + BF16_SUBLANES
NA_RB = 8
NA_WIN = 16
NA_HPB = 4
NA_LANES = NA_HPB * NA_HEAD_DIM
OUT_TM = 512


def _dot(a, b):
    return jnp.dot(a, b, preferred_element_type=jnp.float32)


def _dot_nt(a, b):
    return lax.dot_general(a, b, (((1,), (1,)), ((), ())), preferred_element_type=jnp.float32)


def _sigmoid(x):
    return 1.0 / (1.0 + jnp.exp(-x))


def _silu(x):
    return x / (1.0 + jnp.exp(-x))


def _log_sigmoid(x):
    return jnp.minimum(x, 0.0) - jnp.log1p(jnp.exp(-jnp.abs(x)))


def _proj_kernel(x_ref, xp_ref, xn_ref_, nw_ref, w_ref, wgt_ref, bgt_ref, cw_ref, cb_ref,
                 o_ref, vt_ref, gt_ref, xn_s, acc_s):
    i = pl.program_id(1)
    j = pl.program_id(2)
    ni = pl.num_programs(1)
    tm = x_ref.shape[1]

    def norm(x):
        y = x * lax.rsqrt(jnp.mean(x * x, axis=-1, keepdims=True) + EPS)
        return y * nw_ref[...]

    @pl.when(j == 0)
    def _():
        cur = norm(x_ref[0]).astype(jnp.bfloat16)
        prev = jnp.where(i > 0, norm(xp_ref[0]), 0.0).astype(jnp.bfloat16)
        nxt = jnp.where(i < ni - 1, norm(xn_ref_[0]), 0.0).astype(jnp.bfloat16)
        xn_s[0:HALO, :] = prev
        xn_s[HALO:HALO + tm, :] = cur
        xn_s[HALO + tm:2 * HALO + tm, :] = nxt
        gt_ref[0] = _dot_nt(wgt_ref[...], cur) + bgt_ref[...]

    @pl.when(j <= STEP_MK)
    def _():
        acc_s[...] = _dot(xn_s[...], w_ref[...])
        pad = CONV_W // 2
        y = cb_ref[...]
        for t in range(CONV_W):
            y = y + acc_s[pl.ds(HALO - pad + t, tm), :] * cw_ref[t:t + 1, :]
        y = _silu(y)
        y = y * jnp.where(j == STEP_MK, M_HEAD_DIM ** -0.5, 1.0)
        o_ref[0] = y.astype(o_ref.dtype)

    @pl.when(j == STEP_MV)
    def _():
        vt_ref[0] = _dot_nt(w_ref[...], xn_s[HALO:HALO + tm, :]).astype(vt_ref.dtype)

    @pl.when(j > STEP_MV)
    def _():
        o_ref[0] = _dot(xn_s[HALO:HALO + tm, :], w_ref[...]).astype(o_ref.dtype)


def _proj(x, norm_w, w_steps, w_gt, b_gt, conv_w, conv_b):
    B, T, D = x.shape
    tm = min(PROJ_TM, T)
    assert T % tm == 0 and tm % HALO == 0
    ni = T // tm
    hb = tm // HALO
    return pl.pallas_call(
        _proj_kernel,
        grid=(B, ni, N_STEPS),
        in_specs=[
            pl.BlockSpec((1, tm, D), lambda b, i, j: (b, i, 0)),
            pl.BlockSpec((1, HALO, D), lambda b, i, j: (b, jnp.maximum(i * hb - 1, 0), 0)),
            pl.BlockSpec((1, HALO, D), lambda b, i, j: (b, jnp.minimum((i + 1) * hb, T // HALO - 1), 0)),
            pl.BlockSpec((1, D), lambda b, i, j: (0, 0)),
            pl.BlockSpec((D, D_MODEL), lambda b, i, j: (0, j)),
            pl.BlockSpec((N_GATES, D), lambda b, i, j: (0, 0)),
            pl.BlockSpec((N_GATES, 1), lambda b, i, j: (0, 0)),
            pl.BlockSpec((SUBLANES, D_MODEL), lambda b, i, j: (0, jnp.minimum(j, STEP_MK))),
            pl.BlockSpec((1, D_MODEL), lambda b, i, j: (0, jnp.minimum(j, STEP_MK))),
        ],
        out_specs=[
            pl.BlockSpec((1, tm, D_MODEL), lambda b, i, j: (b, i, jnp.where(j < STEP_MV, j, j - 1))),
            pl.BlockSpec((1, M_WIDTH, tm), lambda b, i, j: (b, 0, i)),
            pl.BlockSpec((1, N_GATES, tm), lambda b, i, j: (b, 0, i)),
        ],
        out_shape=[
            jax.ShapeDtypeStruct((B, T, D_MAIN), jnp.bfloat16),
            jax.ShapeDtypeStruct((B, M_WIDTH, T), jnp.bfloat16),
            jax.ShapeDtypeStruct((B, N_GATES, T), jnp.float32),
        ],
        scratch_shapes=[
            pltpu.VMEM((tm + 2 * HALO, D), jnp.bfloat16),
            pltpu.VMEM((tm + 2 * HALO, D_MODEL), jnp.float32),
        ],
        compiler_params=pltpu.CompilerParams(
            dimension_semantics=("arbitrary", "arbitrary", "arbitrary"),
            vmem_limit_bytes=VMEM_LIMIT_BYTES),
        name="proj",
    )(x, x, x, norm_w, w_steps, w_gt, b_gt, conv_w, conv_b)


def _scan_lanes(x, pos, L, d, op, fill):
    n = x.shape[1]
    sh = 1
    while sh < L:
        if d == 0:
            x = op(x, jnp.where(pos >= sh, pltpu.roll(x, sh, axis=1), fill))
        else:
            x = op(x, jnp.where(pos < L - sh, pltpu.roll(x, n - sh, axis=1), fill))
        sh *= 2
    return x


def _gates_kernel(L, gt_ref, a_ref, b_ref, cm_ref, at_ref):
    n = gt_ref.shape[2]
    pos = lax.broadcasted_iota(jnp.int32, (SUBLANES, n), 1) % L
    a_rows = []
    for d in range(2):
        rows = slice(d * SUBLANES, (d + 1) * SUBLANES)
        g8 = gt_ref[0, rows, :]
        b8 = _scan_lanes(_log_sigmoid(g8), pos, L, d, jnp.add, 0.0)
        b8 = pltpu.roll(b8, M_HEADS, axis=0)
        a8 = g8 - b8
        a_ref[0, rows, :] = a8
        b_ref[0, rows, :] = b8
        cm_ref[0, rows, :] = _scan_lanes(a8, pos, L, d, jnp.maximum, -jnp.inf)
        a_rows.append(a8)
    pad = jnp.zeros((LANES - 2 * SUBLANES, n), jnp.float32)
    at_ref[0] = jnp.concatenate(a_rows + [pad], axis=0).T


def _gates(g_t, L):
    B, _, T = g_t.shape
    tg = min(GATES_TT, T)
    assert T % tg == 0 and tg % L == 0
    rows = pl.BlockSpec((1, N_GATES, tg), lambda b, i: (b, 0, i))
    sds = jax.ShapeDtypeStruct((B, N_GATES, T), jnp.float32)
    return pl.pallas_call(
        functools.partial(_gates_kernel, L),
        grid=(B, T // tg),
        in_specs=[rows],
        out_specs=[rows, rows, rows, pl.BlockSpec((1, tg, LANES), lambda b, i: (b, i, 0))],
        out_shape=[sds, sds, sds, jax.ShapeDtypeStruct((B, T, LANES), jnp.float32)],
        compiler_params=pltpu.CompilerParams(
            dimension_semantics=("arbitrary", "arbitrary"),
            vmem_limit_bytes=VMEM_LIMIT_BYTES),
        name="gates",
    )(g_t)


def _mlstm_kernel(qf_ref, kf_ref, vtf_ref, af_ref, bf_ref, cmf_ref, atf_ref,
                  qb_ref, kb_ref, vtb_ref, ab_ref, bb_ref, cmb_ref, atb_ref,
                  hf_ref, hb_ref, st_ref, m_ref):
    L = qf_ref.shape[1]
    bf16 = jnp.bfloat16

    @pl.when(pl.program_id(1) == 0)
    def _():
        st_ref[...] = jnp.zeros_like(st_ref)
        m_ref[...] = jnp.zeros_like(m_ref)

    row = lax.broadcasted_iota(jnp.int32, (L, L), 0)
    col = lax.broadcasted_iota(jnp.int32, (L, L), 1)
    ones_rows = jnp.ones((BF16_SUBLANES, L), bf16)
    dirs = ((qf_ref, kf_ref, vtf_ref, af_ref, bf_ref, cmf_ref, atf_ref, hf_ref),
            (qb_ref, kb_ref, vtb_ref, ab_ref, bb_ref, cmb_ref, atb_ref, hb_ref))
    pre = []
    for d, (_, _, _, a_ref, b_ref, cm_ref, _, _) in enumerate(dirs):
        rows = slice(d * SUBLANES, (d + 1) * SUBLANES)
        last = L - 1 if d == 0 else 0
        a8, b8 = a_ref[0, rows, :], b_ref[0, rows, :]
        m8 = m_ref[d]
        c8 = jnp.maximum(cm_ref[0, rows, :], m8)
        c_last = c8[:, last:last + 1]
        m_ref[d] = b8[:, last:last + 1] + c_last
        pre.append(dict(
            c=c8, w=jnp.exp(m8 - c8), floor=jnp.exp(-b8 - c8), wk=jnp.exp(a8 - c_last).astype(bf16),
            decay=jnp.exp(m8 - c_last),
            mask=(row <= col) if d == 0 else (row >= col)))

    for h in range(M_HEADS):
        for d, (q_ref, k_ref, vt_ref, _, _, _, at_ref, ht_ref) in enumerate(dirs):
            p = pre[d]
            idx = d * M_HEADS + h
            lanes = slice(h * M_HEAD_DIM, (h + 1) * M_HEAD_DIM)
            q = q_ref[0, :, lanes]
            k = k_ref[0, :, lanes]
            vt = jnp.concatenate([vt_ref[0, lanes, :], ones_rows], axis=0)
            st = st_ref[idx]
            a_col = at_ref[0, :, d * SUBLANES + h:d * SUBLANES + h + 1]

            e = jnp.exp(jnp.where(p["mask"], a_col - p["c"][h:h + 1, :], -jnp.inf))
            pt = (_dot_nt(k, q) * e).astype(bf16)
            num = _dot(vt, pt) + p["w"][h:h + 1, :] * _dot_nt(st.astype(bf16), q)
            den = num[M_HEAD_DIM:M_HEAD_DIM + 1, :]
            ht_ref[0, lanes, :] = num[:M_HEAD_DIM] / jnp.maximum(jnp.abs(den), p["floor"][h:h + 1, :])

    for h in range(M_HEADS):
        for d, (_, k_ref, vt_ref, _, _, _, _, _) in enumerate(dirs):
            p = pre[d]
            idx = d * M_HEADS + h
            lanes = slice(h * M_HEAD_DIM, (h + 1) * M_HEAD_DIM)
            vt = jnp.concatenate([vt_ref[0, lanes, :], ones_rows], axis=0)
            st_ref[idx] = (p["decay"][h:h + 1, :] * st_ref[idx]
                           + _dot(vt * p["wk"][h:h + 1, :], k_ref[0, :, lanes]))


def _mlstm(proj, v_t, gates):
    B, T, _ = proj.shape
    L = min(MLSTM_L, T)
    assert T % L == 0
    nc = T // L
    a_r, b_r, cm_r, a_t = gates

    def tok(width, col, rev):
        return pl.BlockSpec((1, L, width), lambda b, c: (b, nc - 1 - c if rev else c, col))

    def feat(rows, rev):
        return pl.BlockSpec((1, rows, L), lambda b, c: (b, 0, nc - 1 - c if rev else c))

    def side(rev):
        return [tok(M_WIDTH, COL_MQ, rev), tok(M_WIDTH, COL_MK, rev), feat(M_WIDTH, rev),
                feat(N_GATES, rev), feat(N_GATES, rev), feat(N_GATES, rev), tok(LANES, 0, rev)]

    return pl.pallas_call(
        _mlstm_kernel,
        grid=(B, nc),
        in_specs=side(False) + side(True),
        out_specs=[feat(M_WIDTH, False), feat(M_WIDTH, True)],
        out_shape=[
            jax.ShapeDtypeStruct((B, M_WIDTH, T), jnp.float32),
            jax.ShapeDtypeStruct((B, M_WIDTH, T), jnp.float32),
        ],
        scratch_shapes=[
            pltpu.VMEM((2 * M_HEADS, M_AUG, M_HEAD_DIM), jnp.float32),
            pltpu.VMEM((2, SUBLANES, 1), jnp.float32),
        ],
        compiler_params=pltpu.CompilerParams(
            dimension_semantics=("arbitrary", "arbitrary"),
            vmem_limit_bytes=VMEM_LIMIT_BYTES),
        name="mlstm",
    )(proj, proj, v_t, a_r, b_r, cm_r, a_t, proj, proj, v_t, a_r, b_r, cm_r, a_t)


_NA_KB = tuple(int(v) for v in np.clip(np.arange(GRID_W // NA_QCB) * NA_QCB - NA_KW // 2, 0, GRID_W - NA_KCB))
NA_NCB = GRID_W // NA_QCB
NA_MQ = NA_RB * NA_QCB
NA_NK = NA_WIN * NA_KCB


def _na_bias_table(rpb):
    n_dr, n_dc = 2 * NA_MAX_KH - 1, 2 * NA_KW - 1
    i = np.arange(NA_RB)[:, None]
    jr = np.arange(NA_WIN)[None, :]
    row_sel = np.zeros((3, NA_RB, NA_WIN, n_dr), np.float32)
    for rt in range(3):
        off = (0, -NA_RB // 2, -NA_RB)[rt]
        if rt == 0:
            rs = np.maximum(i - NA_MAX_KH // 2, 0)
        elif rt == 1:
            rs = i - NA_MAX_KH // 2
        else:
            rs = np.minimum(i - NA_MAX_KH // 2, 0)
        krow = off + jr
        ii, jj = np.nonzero((krow >= rs) & (krow < rs + NA_MAX_KH))
        row_sel[rt, ii, jj, (krow - i)[ii, jj] + NA_MAX_KH - 1] = 1.0
    qc = np.arange(NA_QCB)[:, None]
    kc = np.arange(NA_KCB)[None, :]
    col_sel = np.zeros((NA_NCB, NA_QCB, NA_KCB, n_dc), np.float32)
    for n in range(NA_NCB):
        qcol = n * NA_QCB + qc
        kcol = _NA_KB[n] + kc
        cs = np.clip(qcol - NA_KW // 2, 0, GRID_W - NA_KW)
        qq, kk = np.nonzero((kcol >= cs) & (kcol < cs + NA_KW))
        col_sel[n, qq, kk, (kcol - qcol)[qq, kk] + NA_KW - 1] = 1.0
    ok = (np.einsum("xijr->xij", row_sel)[:, None, :, None, :, None]
          * np.einsum("nqkc->nqk", col_sel)[None, :, None, :, None, :]) > 0
    hi = lax.Precision.HIGHEST
    r = rpb.astype(jnp.float32).reshape(NA_HEADS // NA_HPB, NA_HPB, n_dr, n_dc)
    a = jnp.einsum("ghrc,xijr->gxhijc", r, row_sel, precision=hi)
    t = jnp.einsum("gxhijc,nqkc->gxnhiqjk", a, col_sel, precision=hi)
    t = jnp.where(ok[None, :, :, None], t, NEG)
    return t.reshape(NA_HEADS // NA_HPB, 3 * NA_NCB, NA_HPB * NA_MQ, NA_NK)


def _natten_kernel(q_ref, k0, k1, k2, k3, v0, v1, v2, v3, z_ref, tab_ref, o_ref, acc_s):
    rb = pl.program_id(2)
    nrb = pl.num_programs(2)
    rt = jnp.where(rb == 0, 0, jnp.where(rb == nrb - 1, 2, 1))
    kf = jnp.concatenate([r[0].astype(jnp.float32) for r in (k0, k1, k2, k3)], axis=0)
    vf = jnp.concatenate([r[0].astype(jnp.float32) for r in (v0, v1, v2, v3)], axis=0)
    q = q_ref[0]
    lane = lax.broadcasted_iota(jnp.int32, (NA_MQ, NA_LANES), 1) // NA_HEAD_DIM
    for n in range(NA_NCB):
        kb = _NA_KB[n]
        kn = jnp.concatenate([kf[j * GRID_W + kb:j * GRID_W + kb + NA_KCB] for j in range(NA_WIN)],
                             axis=0).astype(jnp.bfloat16)
        vn = jnp.concatenate([vf[j * GRID_W + kb:j * GRID_W + kb + NA_KCB] for j in range(NA_WIN)],
                             axis=0).astype(jnp.bfloat16)
        qn = jnp.concatenate([q[i * GRID_W + n * NA_QCB:i * GRID_W + (n + 1) * NA_QCB] for i in range(NA_RB)],
                             axis=0)
        qn = qn * (NA_HEAD_DIM ** -0.5)
        qs = jnp.concatenate([jnp.where(lane == hh, qn, 0.0).astype(jnp.bfloat16) for hh in range(NA_HPB)],
                             axis=0)
        s = _dot_nt(qs, kn) + tab_ref[0, rt * NA_NCB + n]
        m = jnp.max(s, axis=1, keepdims=True)
        e = jnp.exp(s - m)
        l = jnp.sum(e, axis=1, keepdims=True)
        o = _dot(e.astype(jnp.bfloat16), vn) / l
        on = jnp.zeros((NA_MQ, NA_LANES), jnp.float32)
        for hh in range(NA_HPB):
            on = jnp.where(lane == hh, o[hh * NA_MQ:(hh + 1) * NA_MQ], on)
        for i in range(NA_RB):
            acc_s[i * GRID_W + n * NA_QCB:i * GRID_W + (n + 1) * NA_QCB, :] = on[i * NA_QCB:(i + 1) * NA_QCB]
    o_ref[0] = (acc_s[...] * _silu(z_ref[0].astype(jnp.float32))).astype(o_ref.dtype)


def _natten(proj, table):
    B, T, _ = proj.shape
    rows = T // GRID_W
    assert T % GRID_W == 0 and rows % NA_RB == 0 and rows >= NA_WIN
    nrb = rows // NA_RB
    ng = NA_HEADS // NA_HPB
    tq = NA_RB * GRID_W
    pr = NA_WIN // 4
    tp = pr * GRID_W
    cpt = D_MODEL // NA_LANES

    def win(col, p):
        def index(g, b, rb):
            start = jnp.clip(rb * (NA_RB // pr) - NA_RB // 2 // pr, 0, rows // pr - NA_WIN // pr)
            return (b, start + p, col * cpt + g)
        return pl.BlockSpec((1, tp, NA_LANES), index)

    return pl.pallas_call(
        _natten_kernel,
        grid=(ng, B, nrb),
        in_specs=[pl.BlockSpec((1, tq, NA_LANES), lambda g, b, rb: (b, rb, COL_NQ * cpt + g))]
        + [win(COL_NK, p) for p in range(4)]
        + [win(COL_NV, p) for p in range(4)]
        + [pl.BlockSpec((1, tq, NA_LANES), lambda g, b, rb: (b, rb, COL_NZ * cpt + g)),
           pl.BlockSpec((1, 3 * NA_NCB, NA_HPB * NA_MQ, NA_NK), lambda g, b, rb: (g, 0, 0, 0))],
        out_specs=pl.BlockSpec((1, tq, NA_LANES), lambda g, b, rb: (b, rb, g)),
        out_shape=jax.ShapeDtypeStruct((B, T, NA_WIDTH), jnp.bfloat16),
        scratch_shapes=[pltpu.VMEM((tq, NA_LANES), jnp.float32)],
        compiler_params=pltpu.CompilerParams(
            dimension_semantics=("arbitrary", "arbitrary", "arbitrary"),
            vmem_limit_bytes=VMEM_LIMIT_BYTES),
        name="natten",
    )(proj, *([proj] * 8), proj, table)


def _out_kernel(x_ref, hf_ref, hb_ref, o_ref, z_ref, yb_ref, ga_ref, gb_ref, mhw_ref, wda_ref, wdb_ref, wo_ref,
                fnw_ref, y_ref):
    f32 = jnp.float32
    h = (hf_ref[0] + hb_ref[0]).T * _sigmoid(o_ref[0].astype(f32))
    parts = []
    for hd in range(M_HEADS):
        hh = h[:, hd * M_HEAD_DIM:(hd + 1) * M_HEAD_DIM]
        parts.append(hh * lax.rsqrt(jnp.mean(hh * hh, axis=-1, keepdims=True) + EPS))
    hn = jnp.concatenate(parts, axis=-1) * mhw_ref[...]
    ya = (hn * _silu(z_ref[0].astype(f32))).astype(jnp.bfloat16)
    merged = (_sigmoid(ga_ref[0].astype(f32)) * _dot(ya, wda_ref[...])
              + _sigmoid(gb_ref[0].astype(f32)) * _dot(yb_ref[0], wdb_ref[...]))
    y = x_ref[0] + _dot(merged.astype(jnp.bfloat16), wo_ref[...])
    y = y * lax.rsqrt(jnp.mean(y * y, axis=-1, keepdims=True) + EPS)
    y_ref[0] = y * fnw_ref[...]


def _out(x, h_f, h_b, proj, y_b, mh_w, w_da, w_db, w_o, fn_w):
    B, T, D = x.shape
    tm = min(OUT_TM, T)
    assert T % tm == 0

    def tok(col=0):
        return pl.BlockSpec((1, tm, D_MODEL), lambda b, i, col=col: (b, i, col))

    def feat():
        return pl.BlockSpec((1, M_WIDTH, tm), lambda b, i: (b, 0, i))

    def const(shape):
        return pl.BlockSpec(shape, lambda b, i: (0,) * len(shape))

    return pl.pallas_call(
        _out_kernel,
        grid=(B, T // tm),
        in_specs=[tok(), feat(), feat(), tok(COL_MO), tok(COL_MZ), tok(), tok(COL_GA), tok(COL_GB),
                  const((1, D)), const((D, D)), const((D, D)), const((D, D)), const((1, D))],
        out_specs=tok(),
        out_shape=jax.ShapeDtypeStruct((B, T, D), jnp.float32),
        compiler_params=pltpu.CompilerParams(
            dimension_semantics=("arbitrary", "arbitrary"),
            vmem_limit_bytes=VMEM_LIMIT_BYTES),
        name="out",
    )(x, h_f, h_b, proj, proj, y_b, proj, proj, mh_w, w_da, w_db, w_o, fn_w)


def _prepare(norm_w, w_in, b_gate, conv_w, conv_b, mh_norm_w, rpb, w_down_a, w_down_b, w_out, final_norm_w):
    bf16 = jnp.bfloat16
    g0 = 5 * M_WIDTH
    v0 = 2 * M_WIDTH
    w_steps = jnp.concatenate([w_in[:, :v0], w_in[:, v0:v0 + M_WIDTH].T, w_in[:, v0 + M_WIDTH:g0],
                               w_in[:, g0 + N_GATES:]], axis=1).astype(bf16)
    w_gt = w_in[:, g0:g0 + N_GATES].T.astype(bf16)
    cw = jnp.pad(conv_w, ((0, SUBLANES - CONV_W), (0, 0)))
    return dict(
        norm_w=norm_w.reshape(1, D_MODEL), w_steps=w_steps, w_gt=w_gt, b_gt=b_gate.reshape(N_GATES, 1),
        conv_w=cw, conv_b=conv_b.reshape(1, 2 * M_WIDTH), mh_w=mh_norm_w.reshape(1, M_WIDTH),
        table=_na_bias_table(rpb), w_da=w_down_a.astype(bf16), w_db=w_down_b.astype(bf16),
        w_o=w_out.astype(bf16), fn_w=final_norm_w.reshape(1, D_MODEL))


def _trunk(x, p):
    proj, v_t, g_t = _proj(x, p["norm_w"], p["w_steps"], p["w_gt"], p["b_gt"], p["conv_w"], p["conv_b"])
    h_f, h_b = _mlstm(proj, v_t, _gates(g_t, min(MLSTM_L, x.shape[1])))
    y_b = _natten(proj, p["table"])
    return _out(x, h_f, h_b, proj, y_b, p["mh_w"], p["w_da"], p["w_db"], p["w_o"], p["fn_w"])


def kernel(x_prompt, x_sample, norm_w, w_in, b_gate, conv_w, conv_b, mh_norm_w, rpb, w_down_a, w_down_b, w_out,
           final_norm_w):
    assert norm_w.shape[0] == 1, "single-layer trunk"
    p = _prepare(norm_w[0], w_in[0], b_gate[0], conv_w[0], conv_b[0], mh_norm_w[0], rpb[0],
                 w_down_a[0], w_down_b[0], w_out[0], final_norm_w)
    return (_trunk(x_prompt, p), _trunk(x_sample, p))
```

```python
import functools

import numpy as np
import jax
import jax.numpy as jnp
from jax import lax
from jax.experimental import pallas as pl
from jax.experimental.pallas import tpu as pltpu

D_MODEL = 1024
GRID_W = 64
M_HEADS = 4
M_HEAD_DIM = 256
M_WIDTH = M_HEADS * M_HEAD_DIM
CONV_W = 5
N_GATES = 4 * M_HEADS
NA_HEADS = 16
NA_HEAD_DIM = 64
NA_WIDTH = NA_HEADS * NA_HEAD_DIM
NA_MAX_KH = 8
NA_KW = 16
NA_QCB = 16
NA_KCB = 32
EPS = 1e-6
NEG = -1e30
LOG2E = 1.4426950408889634

LANES = 128
SUBLANES = 8
BF16_SUBLANES = 16
VMEM_LIMIT_BYTES = 56 * 1024 * 1024

STEP_MV, STEP_MQ, STEP_MK = 0, 1, 2
N_STEPS = 11
COL_MO, COL_MZ, COL_NQ, COL_NK, COL_NV, COL_NZ, COL_GA, COL_GB = range(8)
D_MAIN = 8 * D_MODEL

PROJ_TM = 1024
HALO = BF16_SUBLANES
CONV_PIECE = 256
MLSTM_L = 256
GATES_TT = 2048
M_AUG = M_HEAD_DIM + BF16_SUBLANES
NA_RB = 8
NA_WIN = 16
NA_HPB = 4
NA_LANES = NA_HPB * NA_HEAD_DIM
OUT_TM = 512
OUT_SUB = 256


def _dot(a, b):
    return jnp.dot(a, b, preferred_element_type=jnp.float32)


def _dot_nt(a, b):
    return lax.dot_general(a, b, (((1,), (1,)), ((), ())), preferred_element_type=jnp.float32)


def _sigmoid(x):
    return 1.0 / (1.0 + jnp.exp(-x))


def _silu(x):
    return x / (1.0 + jnp.exp(-x))


def _log_sigmoid(x):
    return jnp.minimum(x, 0.0) - jnp.log1p(jnp.exp(-jnp.abs(x)))


def _proj_kernel(x_ref, xp_ref, xn_ref_, nw_ref, w_ref, wgt_ref, bgt_ref, cw_ref, cb_ref,
                 o_ref, q_ref, k_ref, vt_ref, gt_ref, xn_s, acc_s):
    i = pl.program_id(1)
    j = pl.program_id(2)
    ni = pl.num_programs(1)
    tm = x_ref.shape[1]
    pad = CONV_W // 2

    def norm(x):
        y = x * lax.rsqrt(jnp.mean(x * x, axis=-1, keepdims=True) + EPS)
        return y * nw_ref[...]

    def conv_tile(out_ref, scale):
        for c in range(tm // CONV_PIECE):
            r0 = c * CONV_PIECE
            acc_s[c % 2] = _dot(xn_s[r0:r0 + CONV_PIECE + 2 * HALO, :], w_ref[...])
            y = cb_ref[...]
            for t in range(CONV_W):
                y = y + acc_s[c % 2, pl.ds(HALO - pad + t, CONV_PIECE), :] * cw_ref[t:t + 1, :]
            y = _silu(y).astype(out_ref.dtype)
            out_ref[0, r0:r0 + CONV_PIECE, :] = y if scale is None else y * scale

    @pl.when(j == STEP_MV)
    def _():
        cur = norm(x_ref[0]).astype(jnp.bfloat16)
        prev = jnp.where(i > 0, norm(xp_ref[0]), 0.0).astype(jnp.bfloat16)
        nxt = jnp.where(i < ni - 1, norm(xn_ref_[0]), 0.0).astype(jnp.bfloat16)
        xn_s[0:HALO, :] = prev
        xn_s[HALO:HALO + tm, :] = cur
        xn_s[HALO + tm:2 * HALO + tm, :] = nxt
        gt_ref[0] = _dot_nt(wgt_ref[...], cur) + bgt_ref[...]
        vt_ref[0] = _dot_nt(w_ref[...], cur).astype(vt_ref.dtype)

    @pl.when(j == STEP_MQ)
    def _():
        conv_tile(q_ref, None)

    @pl.when(j == STEP_MK)
    def _():
        conv_tile(k_ref, M_HEAD_DIM ** -0.5)

    @pl.when(j > STEP_MK)
    def _():
        o_ref[0] = _dot(xn_s[HALO:HALO + tm, :], w_ref[...]).astype(o_ref.dtype)


def _proj(x, norm_w, w_steps, w_gt, b_gt, conv_w, conv_b):
    B, T, D = x.shape
    tm = min(PROJ_TM, T)
    assert T % tm == 0 and tm % CONV_PIECE == 0
    ni = T // tm
    hb = tm // HALO
    tile = pl.BlockSpec((1, tm, D_MODEL), lambda b, i, j: (b, i, 0))
    return pl.pallas_call(
        _proj_kernel,
        grid=(B, ni, N_STEPS),
        in_specs=[
            pl.BlockSpec((1, tm, D), lambda b, i, j: (b, i, 0)),
            pl.BlockSpec((1, HALO, D), lambda b, i, j: (b, jnp.maximum(i * hb - 1, 0), 0)),
            pl.BlockSpec((1, HALO, D), lambda b, i, j: (b, jnp.minimum((i + 1) * hb, T // HALO - 1), 0)),
            pl.BlockSpec((1, D), lambda b, i, j: (0, 0)),
            pl.BlockSpec((D, D_MODEL), lambda b, i, j: (0, j)),
            pl.BlockSpec((N_GATES, D), lambda b, i, j: (0, 0)),
            pl.BlockSpec((N_GATES, 1), lambda b, i, j: (0, 0)),
            pl.BlockSpec((SUBLANES, D_MODEL), lambda b, i, j: (0, jnp.clip(j - STEP_MQ, 0, 1))),
            pl.BlockSpec((1, D_MODEL), lambda b, i, j: (0, jnp.clip(j - STEP_MQ, 0, 1))),
        ],
        out_specs=[
            pl.BlockSpec((1, tm, D_MODEL), lambda b, i, j: (b, i, jnp.maximum(j - STEP_MK - 1, 0))),
            tile, tile,
            pl.BlockSpec((1, M_WIDTH, tm), lambda b, i, j: (b, 0, i)),
            pl.BlockSpec((1, N_GATES, tm), lambda b, i, j: (b, 0, i)),
        ],
        out_shape=[
            jax.ShapeDtypeStruct((B, T, D_MAIN), jnp.bfloat16),
            jax.ShapeDtypeStruct((B, T, M_WIDTH), jnp.bfloat16),
            jax.ShapeDtypeStruct((B, T, M_WIDTH), jnp.bfloat16),
            jax.ShapeDtypeStruct((B, M_WIDTH, T), jnp.bfloat16),
            jax.ShapeDtypeStruct((B, N_GATES, T), jnp.float32),
        ],
        scratch_shapes=[
            pltpu.VMEM((tm + 2 * HALO, D), jnp.bfloat16),
            pltpu.VMEM((2, CONV_PIECE + 2 * HALO, D_MODEL), jnp.float32),
        ],
        compiler_params=pltpu.CompilerParams(
            dimension_semantics=("arbitrary", "arbitrary", "arbitrary"),
            vmem_limit_bytes=VMEM_LIMIT_BYTES),
        name="proj",
    )(x, x, x, norm_w, w_steps, w_gt, b_gt, conv_w, conv_b)


def _scan_lanes(x, pos, L, d, op, fill):
    n = x.shape[1]
    sh = 1
    while sh < L:
        if d == 0:
            x = op(x, jnp.where(pos >= sh, pltpu.roll(x, sh, axis=1), fill))
        else:
            x = op(x, jnp.where(pos < L - sh, pltpu.roll(x, n - sh, axis=1), fill))
        sh *= 2
    return x


def _gates_kernel(L, gt_ref, a_ref, b_ref, cm_ref, at_ref):
    n = gt_ref.shape[2]
    pos = lax.broadcasted_iota(jnp.int32, (SUBLANES, n), 1) % L
    a_rows = []
    for d in range(2):
        rows = slice(d * SUBLANES, (d + 1) * SUBLANES)
        g8 = gt_ref[0, rows, :]
        b8 = _scan_lanes(_log_sigmoid(g8), pos, L, d, jnp.add, 0.0)
        b8 = pltpu.roll(b8, M_HEADS, axis=0)
        a8 = g8 - b8
        a_ref[0, rows, :] = a8
        b_ref[0, rows, :] = b8
        cm_ref[0, rows, :] = _scan_lanes(a8, pos, L, d, jnp.maximum, -jnp.inf)
        a_rows.append(a8)
    pad = jnp.zeros((LANES - 2 * SUBLANES, n), jnp.float32)
    at_ref[0] = jnp.concatenate(a_rows + [pad], axis=0).T


def _gates(g_t, L):
    B, _, T = g_t.shape
    tg = min(GATES_TT, T)
    assert T % tg == 0 and tg % L == 0
    rows = pl.BlockSpec((1, N_GATES, tg), lambda b, i: (b, 0, i))
    sds = jax.ShapeDtypeStruct((B, N_GATES, T), jnp.float32)
    return pl.pallas_call(
        functools.partial(_gates_kernel, L),
        grid=(B, T // tg),
        in_specs=[rows],
        out_specs=[rows, rows, rows, pl.BlockSpec((1, tg, LANES), lambda b, i: (b, i, 0))],
        out_shape=[sds, sds, sds, jax.ShapeDtypeStruct((B, T, LANES), jnp.float32)],
        compiler_params=pltpu.CompilerParams(
            dimension_semantics=("arbitrary", "arbitrary"),
            vmem_limit_bytes=VMEM_LIMIT_BYTES),
        name="gates",
    )(g_t)


def _mlstm_kernel(qf_ref, kf_ref, vtf_ref, af_ref, bf_ref, cmf_ref, atf_ref,
                  qb_ref, kb_ref, vtb_ref, ab_ref, bb_ref, cmb_ref, atb_ref,
                  hf_ref, hb_ref, st_ref, m_ref):
    L = qf_ref.shape[1]
    bf16 = jnp.bfloat16

    @pl.when(pl.program_id(1) == 0)
    def _():
        st_ref[...] = jnp.zeros_like(st_ref)
        m_ref[...] = jnp.zeros_like(m_ref)

    row = lax.broadcasted_iota(jnp.int32, (L, L), 0)
    col = lax.broadcasted_iota(jnp.int32, (L, L), 1)
    ones_rows = jnp.ones((BF16_SUBLANES, L), bf16)
    dirs = ((qf_ref, kf_ref, vtf_ref, af_ref, bf_ref, cmf_ref, atf_ref, hf_ref),
            (qb_ref, kb_ref, vtb_ref, ab_ref, bb_ref, cmb_ref, atb_ref, hb_ref))
    pre = []
    for d, (_, _, _, a_ref, b_ref, cm_ref, _, _) in enumerate(dirs):
        rows = slice(d * SUBLANES, (d + 1) * SUBLANES)
        last = L - 1 if d == 0 else 0
        a8, b8 = a_ref[0, rows, :], b_ref[0, rows, :]
        m8 = m_ref[d]
        c8 = jnp.maximum(cm_ref[0, rows, :], m8)
        c_last = c8[:, last:last + 1]
        m_ref[d] = b8[:, last:last + 1] + c_last
        pre.append(dict(
            c=c8, w=jnp.exp(m8 - c8), floor=jnp.exp(-b8 - c8), wk=jnp.exp(a8 - c_last).astype(bf16),
            decay=jnp.exp(m8 - c_last),
            mask=(row <= col) if d == 0 else (row >= col)))

    for h in range(M_HEADS):
        for d, (q_ref, k_ref, vt_ref, _, _, _, at_ref, ht_ref) in enumerate(dirs):
            p = pre[d]
            idx = d * M_HEADS + h
            lanes = slice(h * M_HEAD_DIM, (h + 1) * M_HEAD_DIM)
            q = q_ref[0, :, lanes]
            k = k_ref[0, :, lanes]
            vt = jnp.concatenate([vt_ref[0, lanes, :], ones_rows], axis=0)
            st = st_ref[idx]
            a_col = at_ref[0, :, d * SUBLANES + h:d * SUBLANES + h + 1]

            e = jnp.exp(jnp.where(p["mask"], a_col - p["c"][h:h + 1, :], -jnp.inf))
            pt = (_dot_nt(k, q) * e).astype(bf16)
            num = _dot(vt, pt) + p["w"][h:h + 1, :] * _dot_nt(st.astype(bf16), q)
            den = num[M_HEAD_DIM:M_HEAD_DIM + 1, :]
            ht_ref[0, lanes, :] = (num[:M_HEAD_DIM]
                                   / jnp.maximum(jnp.abs(den), p["floor"][h:h + 1, :])).astype(ht_ref.dtype)

    for h in range(M_HEADS):
        for d, (_, k_ref, vt_ref, _, _, _, _, _) in enumerate(dirs):
            p = pre[d]
            idx = d * M_HEADS + h
            lanes = slice(h * M_HEAD_DIM, (h + 1) * M_HEAD_DIM)
            vt = jnp.concatenate([vt_ref[0, lanes, :], ones_rows], axis=0)
            st_ref[idx] = (p["decay"][h:h + 1, :] * st_ref[idx]
                           + _dot(vt * p["wk"][h:h + 1, :], k_ref[0, :, lanes]))


def _mlstm(q, k, v_t, gates):
    B, T, _ = q.shape
    L = min(MLSTM_L, T)
    assert T % L == 0
    nc = T // L
    a_r, b_r, cm_r, a_t = gates

    def tok(width, col, rev):
        return pl.BlockSpec((1, L, width), lambda b, c: (b, nc - 1 - c if rev else c, col))

    def feat(rows, rev):
        return pl.BlockSpec((1, rows, L), lambda b, c: (b, 0, nc - 1 - c if rev else c))

    def side(rev):
        return [tok(M_WIDTH, 0, rev), tok(M_WIDTH, 0, rev), feat(M_WIDTH, rev),
                feat(N_GATES, rev), feat(N_GATES, rev), feat(N_GATES, rev), tok(LANES, 0, rev)]

    return pl.pallas_call(
        _mlstm_kernel,
        grid=(B, nc),
        in_specs=side(False) + side(True),
        out_specs=[feat(M_WIDTH, False), feat(M_WIDTH, True)],
        out_shape=[
            jax.ShapeDtypeStruct((B, M_WIDTH, T), jnp.bfloat16),
            jax.ShapeDtypeStruct((B, M_WIDTH, T), jnp.bfloat16),
        ],
        scratch_shapes=[
            pltpu.VMEM((2 * M_HEADS, M_AUG, M_HEAD_DIM), jnp.float32),
            pltpu.VMEM((2, SUBLANES, 1), jnp.float32),
        ],
        compiler_params=pltpu.CompilerParams(
            dimension_semantics=("arbitrary", "arbitrary"),
            vmem_limit_bytes=VMEM_LIMIT_BYTES),
        name="mlstm",
    )(q, k, v_t, a_r, b_r, cm_r, a_t, q, k, v_t, a_r, b_r, cm_r, a_t)


_NA_KB = tuple(int(v) for v in np.clip(np.arange(GRID_W // NA_QCB) * NA_QCB - NA_KW // 2, 0, GRID_W - NA_KCB))
NA_NCB = GRID_W // NA_QCB
NA_MQ = NA_RB * NA_QCB
NA_NK = NA_WIN * NA_KCB


def _na_bias_table(rpb):
    n_dr, n_dc = 2 * NA_MAX_KH - 1, 2 * NA_KW - 1
    i = np.arange(NA_RB)[:, None]
    jr = np.arange(NA_WIN)[None, :]
    row_sel = np.zeros((3, NA_RB, NA_WIN, n_dr), np.float32)
    for rt in range(3):
        off = (0, -NA_RB // 2, -NA_RB)[rt]
        if rt == 0:
            rs = np.maximum(i - NA_MAX_KH // 2, 0)
        elif rt == 1:
            rs = i - NA_MAX_KH // 2
        else:
            rs = np.minimum(i - NA_MAX_KH // 2, 0)
        krow = off + jr
        ii, jj = np.nonzero((krow >= rs) & (krow < rs + NA_MAX_KH))
        row_sel[rt, ii, jj, (krow - i)[ii, jj] + NA_MAX_KH - 1] = 1.0
    qc = np.arange(NA_QCB)[:, None]
    kc = np.arange(NA_KCB)[None, :]
    col_sel = np.zeros((NA_NCB, NA_QCB, NA_KCB, n_dc), np.float32)
    for n in range(NA_NCB):
        qcol = n * NA_QCB + qc
        kcol = _NA_KB[n] + kc
        cs = np.clip(qcol - NA_KW // 2, 0, GRID_W - NA_KW)
        qq, kk = np.nonzero((kcol >= cs) & (kcol < cs + NA_KW))
        col_sel[n, qq, kk, (kcol - qcol)[qq, kk] + NA_KW - 1] = 1.0
    ok = (np.einsum("xijr->xij", row_sel)[:, None, :, None, :, None]
          * np.einsum("nqkc->nqk", col_sel)[None, :, None, :, None, :]) > 0
    hi = lax.Precision.HIGHEST
    r = rpb.astype(jnp.float32).reshape(NA_HEADS // NA_HPB, NA_HPB, n_dr, n_dc)
    a = jnp.einsum("ghrc,xijr->gxhijc", r, row_sel, precision=hi)
    t = jnp.einsum("gxhijc,nqkc->gxnhiqjk", a, col_sel, precision=hi)
    t = jnp.where(ok[None, :, :, None], t * LOG2E, NEG)
    return t.reshape(NA_HEADS // NA_HPB, 3 * NA_NCB, NA_HPB * NA_MQ, NA_NK)


def _natten_kernel(q_ref, k0, k1, k2, k3, v0, v1, v2, v3, z_ref, tab_ref, o_ref, acc_s):
    rb = pl.program_id(2)
    nrb = pl.num_programs(2)
    rt = jnp.where(rb == 0, 0, jnp.where(rb == nrb - 1, 2, 1))
    kf = jnp.concatenate([r[0].astype(jnp.float32) for r in (k0, k1, k2, k3)], axis=0)
    vf = jnp.concatenate([r[0].astype(jnp.float32) for r in (v0, v1, v2, v3)], axis=0)
    q = q_ref[0]
    lane = lax.broadcasted_iota(jnp.int32, (NA_MQ, NA_LANES), 1) // NA_HEAD_DIM

    def keys(xf, n):
        kb = _NA_KB[n]
        return jnp.concatenate([xf[j * GRID_W + kb:j * GRID_W + kb + NA_KCB] for j in range(NA_WIN)],
                               axis=0).astype(jnp.bfloat16)

    scores = []
    for n in range(NA_NCB):
        qn = jnp.concatenate([q[i * GRID_W + n * NA_QCB:i * GRID_W + (n + 1) * NA_QCB] for i in range(NA_RB)],
                             axis=0)
        qs = jnp.concatenate([jnp.where(lane == hh, qn, 0.0).astype(jnp.bfloat16) for hh in range(NA_HPB)],
                             axis=0)
        scores.append(_dot_nt(qs, keys(kf, n)) + tab_ref[0, rt * NA_NCB + n])
    for n in range(NA_NCB):
        s = scores[n]
        m = jnp.max(s, axis=1, keepdims=True)
        e = jnp.exp2(s - m)
        l = jnp.sum(e, axis=1, keepdims=True)
        o = _dot(e.astype(jnp.bfloat16), keys(vf, n)) / l
        on = jnp.zeros((NA_MQ, NA_LANES), jnp.float32)
        for hh in range(NA_HPB):
            on = jnp.where(lane == hh, o[hh * NA_MQ:(hh + 1) * NA_MQ], on)
        for i in range(NA_RB):
            acc_s[i * GRID_W + n * NA_QCB:i * GRID_W + (n + 1) * NA_QCB, :] = on[i * NA_QCB:(i + 1) * NA_QCB]
    o_ref[0] = (acc_s[...] * _silu(z_ref[0].astype(jnp.float32))).astype(o_ref.dtype)


def _natten(proj, table):
    B, T, _ = proj.shape
    rows = T // GRID_W
    assert T % GRID_W == 0 and rows % NA_RB == 0 and rows >= NA_WIN
    nrb = rows // NA_RB
    ng = NA_HEADS // NA_HPB
    tq = NA_RB * GRID_W
    pr = NA_WIN // 4
    tp = pr * GRID_W
    cpt = D_MODEL // NA_LANES

    def win(col, p):
        def index(g, b, rb):
            start = jnp.clip(rb * (NA_RB // pr) - NA_RB // 2 // pr, 0, rows // pr - NA_WIN // pr)
            return (b, start + p, col * cpt + g)
        return pl.BlockSpec((1, tp, NA_LANES), index)

    return pl.pallas_call(
        _natten_kernel,
        grid=(ng, B, nrb),
        in_specs=[pl.BlockSpec((1, tq, NA_LANES), lambda g, b, rb: (b, rb, COL_NQ * cpt + g))]
        + [win(COL_NK, p) for p in range(4)]
        + [win(COL_NV, p) for p in range(4)]
        + [pl.BlockSpec((1, tq, NA_LANES), lambda g, b, rb: (b, rb, COL_NZ * cpt + g)),
           pl.BlockSpec((1, 3 * NA_NCB, NA_HPB * NA_MQ, NA_NK), lambda g, b, rb: (g, 0, 0, 0))],
        out_specs=pl.BlockSpec((1, tq, NA_LANES), lambda g, b, rb: (b, rb, g)),
        out_shape=jax.ShapeDtypeStruct((B, T, NA_WIDTH), jnp.bfloat16),
        scratch_shapes=[pltpu.VMEM((tq, NA_LANES), jnp.float32)],
        compiler_params=pltpu.CompilerParams(
            dimension_semantics=("arbitrary", "arbitrary", "arbitrary"),
            vmem_limit_bytes=VMEM_LIMIT_BYTES),
        name="natten",
    )(proj, *([proj] * 8), proj, table)


def _out_kernel(x_ref, hf_ref, hb_ref, o_ref, z_ref, yb_ref, ga_ref, gb_ref, mhw_ref, wda_ref, wdb_ref, wo_ref,
                fnw_ref, y_ref):
    f32 = jnp.float32
    tm = x_ref.shape[1]
    for r0 in range(0, tm, OUT_SUB):
        rows = slice(r0, r0 + OUT_SUB)
        h = hf_ref[0, :, rows].astype(f32) + hb_ref[0, :, rows].astype(f32)
        h = h.T * _sigmoid(o_ref[0, rows, :].astype(f32))
        parts = []
        for hd in range(M_HEADS):
            hh = h[:, hd * M_HEAD_DIM:(hd + 1) * M_HEAD_DIM]
            parts.append(hh * lax.rsqrt(jnp.mean(hh * hh, axis=-1, keepdims=True) + EPS))
        hn = jnp.concatenate(parts, axis=-1) * mhw_ref[...]
        ya = (hn * _silu(z_ref[0, rows, :].astype(f32))).astype(jnp.bfloat16)
        merged = (_sigmoid(ga_ref[0, rows, :].astype(f32)) * _dot(ya, wda_ref[...])
                  + _sigmoid(gb_ref[0, rows, :].astype(f32)) * _dot(yb_ref[0, rows, :], wdb_ref[...]))
        y = x_ref[0, rows, :] + _dot(merged.astype(jnp.bfloat16), wo_ref[...])
        y = y * lax.rsqrt(jnp.mean(y * y, axis=-1, keepdims=True) + EPS)
        y_ref[0, rows, :] = y * fnw_ref[...]


def _out(x, h_f, h_b, proj, y_b, mh_w, w_da, w_db, w_o, fn_w):
    B, T, D = x.shape
    tm = min(OUT_TM, T)
    assert T % tm == 0

    def tok(col=0):
        return pl.BlockSpec((1, tm, D_MODEL), lambda b, i, col=col: (b, i, col))

    def feat():
        return pl.BlockSpec((1, M_WIDTH, tm), lambda b, i: (b, 0, i))

    def const(shape):
        return pl.BlockSpec(shape, lambda b, i: (0,) * len(shape))

    return pl.pallas_call(
        _out_kernel,
        grid=(B, T // tm),
        in_specs=[tok(), feat(), feat(), tok(COL_MO), tok(COL_MZ), tok(), tok(COL_GA), tok(COL_GB),
                  const((1, D)), const((D, D)), const((D, D)), const((D, D)), const((1, D))],
        out_specs=tok(),
        out_shape=jax.ShapeDtypeStruct((B, T, D), jnp.float32),
        compiler_params=pltpu.CompilerParams(
            dimension_semantics=("arbitrary", "arbitrary"),
            vmem_limit_bytes=VMEM_LIMIT_BYTES),
        name="out",
    )(x, h_f, h_b, proj, proj, y_b, proj, proj, mh_w, w_da, w_db, w_o, fn_w)


def _prepare(norm_w, w_in, b_gate, conv_w, conv_b, mh_norm_w, rpb, w_down_a, w_down_b, w_out, final_norm_w):
    bf16 = jnp.bfloat16
    g0 = 5 * M_WIDTH
    v0 = 2 * M_WIDTH
    nq0 = g0 + N_GATES
    w_steps = jnp.concatenate([w_in[:, v0:v0 + M_WIDTH].T, w_in[:, :v0], w_in[:, v0 + M_WIDTH:g0],
                               w_in[:, nq0:nq0 + NA_WIDTH] * (LOG2E * NA_HEAD_DIM ** -0.5),
                               w_in[:, nq0 + NA_WIDTH:]], axis=1).astype(bf16)
    w_gt = w_in[:, g0:g0 + N_GATES].T.astype(bf16)
    cw = jnp.pad(conv_w, ((0, SUBLANES - CONV_W), (0, 0)))
    return dict(
        norm_w=norm_w.reshape(1, D_MODEL), w_steps=w_steps, w_gt=w_gt, b_gt=b_gate.reshape(N_GATES, 1),
        conv_w=cw, conv_b=conv_b.reshape(1, 2 * M_WIDTH), mh_w=mh_norm_w.reshape(1, M_WIDTH),
        table=_na_bias_table(rpb), w_da=w_down_a.astype(bf16), w_db=w_down_b.astype(bf16),
        w_o=w_out.astype(bf16), fn_w=final_norm_w.reshape(1, D_MODEL))


def _trunk(x, p):
    proj, q, k, v_t, g_t = _proj(x, p["norm_w"], p["w_steps"], p["w_gt"], p["b_gt"], p["conv_w"], p["conv_b"])
    h_f, h_b = _mlstm(q, k, v_t, _gates(g_t, min(MLSTM_L, x.shape[1])))
    y_b = _natten(proj, p["table"])
    return _out(x, h_f, h_b, proj, y_b, p["mh_w"], p["w_da"], p["w_db"], p["w_o"], p["fn_w"])


def kernel(x_prompt, x_sample, norm_w, w_in, b_gate, conv_w, conv_b, mh_norm_w, rpb, w_down_a, w_down_b, w_out,
           final_norm_w):
    assert norm_w.shape[0] == 1, "single-layer trunk"
    p = _prepare(norm_w[0], w_in[0], b_gate[0], conv_w[0], conv_b[0], mh_norm_w[0], rpb[0],
                 w_down_a[0], w_down_b[0], w_out[0], final_norm_w)
    return (_trunk(x_prompt, p), _trunk(x_sample, p))
```

```python
import functools

import numpy as np
import jax
import jax.numpy as jnp
from jax import lax
from jax.experimental import pallas as pl
from jax.experimental.pallas import tpu as pltpu

D_MODEL = 1024
GRID_W = 64
M_HEADS = 4
M_HEAD_DIM = 256
M_WIDTH = M_HEADS * M_HEAD_DIM
CONV_W = 5
N_GATES = 4 * M_HEADS
NA_HEADS = 16
NA_HEAD_DIM = 64
NA_WIDTH = NA_HEADS * NA_HEAD_DIM
NA_MAX_KH = 8
NA_KW = 16
NA_QCB = 16
NA_KCB = 32
EPS = 1e-6
NEG = -1e30
LOG2E = 1.4426950408889634

LANES = 128
SUBLANES = 8
BF16_SUBLANES = 16
VMEM_LIMIT_BYTES = 56 * 1024 * 1024

STEP_MV, STEP_MQ, STEP_MK = 0, 1, 2
N_STEPS = 11
COL_MO, COL_MZ, COL_NQ, COL_NK, COL_NV, COL_NZ, COL_GA, COL_GB = range(8)
D_MAIN = 8 * D_MODEL

PROJ_TM = 1024
HALO = BF16_SUBLANES
CONV_PIECE = 256
MLSTM_L = 256
GATES_TT = 2048
M_AUG = M_HEAD_DIM + BF16_SUBLANES
NA_RB = 8
NA_WIN = 16
NA_BPS = 2
NA_HPB = 4
NA_LANES = NA_HPB * NA_HEAD_DIM
OUT_TM = 512
OUT_SUB = 256


def _dot(a, b):
    return jnp.dot(a, b, preferred_element_type=jnp.float32)


def _dot_nt(a, b):
    return lax.dot_general(a, b, (((1,), (1,)), ((), ())), preferred_element_type=jnp.float32)


def _sigmoid(x):
    return 1.0 / (1.0 + jnp.exp(-x))


def _silu(x):
    return x / (1.0 + jnp.exp(-x))


def _log_sigmoid(x):
    return jnp.minimum(x, 0.0) - jnp.log1p(jnp.exp(-jnp.abs(x)))


def _proj_kernel(x_ref, xp_ref, xn_ref_, nw_ref, w_ref, wgt_ref, bgt_ref, cw_ref, cb_ref,
                 o_ref, q_ref, k_ref, vt_ref, gt_ref, xn_s, acc_s):
    i = pl.program_id(1)
    j = pl.program_id(2)
    ni = pl.num_programs(1)
    tm = x_ref.shape[1]
    pad = CONV_W // 2

    def norm(x):
        y = x * lax.rsqrt(jnp.mean(x * x, axis=-1, keepdims=True) + EPS)
        return y * nw_ref[...]

    def conv_tile(out_ref, scale):
        for c in range(tm // CONV_PIECE):
            r0 = c * CONV_PIECE
            acc_s[c % 2] = _dot(xn_s[r0:r0 + CONV_PIECE + 2 * HALO, :], w_ref[...])
            y = cb_ref[...]
            for t in range(CONV_W):
                y = y + acc_s[c % 2, pl.ds(HALO - pad + t, CONV_PIECE), :] * cw_ref[t:t + 1, :]
            y = _silu(y).astype(out_ref.dtype)
            out_ref[0, r0:r0 + CONV_PIECE, :] = y if scale is None else y * scale

    @pl.when(j == STEP_MV)
    def _():
        cur = norm(x_ref[0]).astype(jnp.bfloat16)
        prev = jnp.where(i > 0, norm(xp_ref[0]), 0.0).astype(jnp.bfloat16)
        nxt = jnp.where(i < ni - 1, norm(xn_ref_[0]), 0.0).astype(jnp.bfloat16)
        xn_s[0:HALO, :] = prev
        xn_s[HALO:HALO + tm, :] = cur
        xn_s[HALO + tm:2 * HALO + tm, :] = nxt
        gt_ref[0] = _dot_nt(wgt_ref[...], cur) + bgt_ref[...]
        vt_ref[0] = _dot_nt(w_ref[...], cur).astype(vt_ref.dtype)

    @pl.when(j == STEP_MQ)
    def _():
        conv_tile(q_ref, None)

    @pl.when(j == STEP_MK)
    def _():
        conv_tile(k_ref, M_HEAD_DIM ** -0.5)

    @pl.when(j > STEP_MK)
    def _():
        o_ref[0] = _dot(xn_s[HALO:HALO + tm, :], w_ref[...]).astype(o_ref.dtype)


def _proj(x, norm_w, w_steps, w_gt, b_gt, conv_w, conv_b):
    B, T, D = x.shape
    tm = min(PROJ_TM, T)
    assert T % tm == 0 and tm % CONV_PIECE == 0
    ni = T // tm
    hb = tm // HALO
    tile = pl.BlockSpec((1, tm, D_MODEL), lambda b, i, j: (b, i, 0))
    return pl.pallas_call(
        _proj_kernel,
        grid=(B, ni, N_STEPS),
        in_specs=[
            pl.BlockSpec((1, tm, D), lambda b, i, j: (b, i, 0)),
            pl.BlockSpec((1, HALO, D), lambda b, i, j: (b, jnp.maximum(i * hb - 1, 0), 0)),
            pl.BlockSpec((1, HALO, D), lambda b, i, j: (b, jnp.minimum((i + 1) * hb, T // HALO - 1), 0)),
            pl.BlockSpec((1, D), lambda b, i, j: (0, 0)),
            pl.BlockSpec((D, D_MODEL), lambda b, i, j: (0, j)),
            pl.BlockSpec((N_GATES, D), lambda b, i, j: (0, 0)),
            pl.BlockSpec((N_GATES, 1), lambda b, i, j: (0, 0)),
            pl.BlockSpec((SUBLANES, D_MODEL), lambda b, i, j: (0, jnp.clip(j - STEP_MQ, 0, 1))),
            pl.BlockSpec((1, D_MODEL), lambda b, i, j: (0, jnp.clip(j - STEP_MQ, 0, 1))),
        ],
        out_specs=[
            pl.BlockSpec((1, tm, D_MODEL), lambda b, i, j: (b, i, jnp.maximum(j - STEP_MK - 1, 0))),
            tile, tile,
            pl.BlockSpec((1, M_WIDTH, tm), lambda b, i, j: (b, 0, i)),
            pl.BlockSpec((1, N_GATES, tm), lambda b, i, j: (b, 0, i)),
        ],
        out_shape=[
            jax.ShapeDtypeStruct((B, T, D_MAIN), jnp.bfloat16),
            jax.ShapeDtypeStruct((B, T, M_WIDTH), jnp.bfloat16),
            jax.ShapeDtypeStruct((B, T, M_WIDTH), jnp.bfloat16),
            jax.ShapeDtypeStruct((B, M_WIDTH, T), jnp.bfloat16),
            jax.ShapeDtypeStruct((B, N_GATES, T), jnp.float32),
        ],
        scratch_shapes=[
            pltpu.VMEM((tm + 2 * HALO, D), jnp.bfloat16),
            pltpu.VMEM((2, CONV_PIECE + 2 * HALO, D_MODEL), jnp.float32),
        ],
        compiler_params=pltpu.CompilerParams(
            dimension_semantics=("arbitrary", "arbitrary", "arbitrary"),
            vmem_limit_bytes=VMEM_LIMIT_BYTES),
        name="proj",
    )(x, x, x, norm_w, w_steps, w_gt, b_gt, conv_w, conv_b)


def _scan_lanes(x, pos, L, d, op, fill):
    n = x.shape[1]
    sh = 1
    while sh < L:
        if d == 0:
            x = op(x, jnp.where(pos >= sh, pltpu.roll(x, sh, axis=1), fill))
        else:
            x = op(x, jnp.where(pos < L - sh, pltpu.roll(x, n - sh, axis=1), fill))
        sh *= 2
    return x


def _gates_kernel(L, gt_ref, abc_ref, at_ref):
    n = gt_ref.shape[2]
    pos = lax.broadcasted_iota(jnp.int32, (SUBLANES, n), 1) % L
    a_rows = []
    for d in range(2):
        rows = slice(d * SUBLANES, (d + 1) * SUBLANES)
        g8 = gt_ref[0, rows, :]
        b8 = _scan_lanes(_log_sigmoid(g8), pos, L, d, jnp.add, 0.0)
        b8 = pltpu.roll(b8, M_HEADS, axis=0)
        a8 = g8 - b8
        abc_ref[0, d * SUBLANES:(d + 1) * SUBLANES, :] = a8
        abc_ref[0, N_GATES + d * SUBLANES:N_GATES + (d + 1) * SUBLANES, :] = b8
        abc_ref[0, 2 * N_GATES + d * SUBLANES:2 * N_GATES + (d + 1) * SUBLANES, :] = _scan_lanes(
            a8, pos, L, d, jnp.maximum, -jnp.inf)
        a_rows.append(a8)
    pad = jnp.zeros((LANES - 2 * SUBLANES, n), jnp.float32)
    at_ref[0] = jnp.concatenate(a_rows + [pad], axis=0).T


def _gates(g_t, L):
    B, _, T = g_t.shape
    tg = min(GATES_TT, T)
    assert T % tg == 0 and tg % L == 0
    return pl.pallas_call(
        functools.partial(_gates_kernel, L),
        grid=(B, T // tg),
        in_specs=[pl.BlockSpec((1, N_GATES, tg), lambda b, i: (b, 0, i))],
        out_specs=[pl.BlockSpec((1, 3 * N_GATES, tg), lambda b, i: (b, 0, i)),
                   pl.BlockSpec((1, tg, LANES), lambda b, i: (b, i, 0))],
        out_shape=[jax.ShapeDtypeStruct((B, 3 * N_GATES, T), jnp.float32),
                   jax.ShapeDtypeStruct((B, T, LANES), jnp.float32)],
        compiler_params=pltpu.CompilerParams(
            dimension_semantics=("arbitrary", "arbitrary"),
            vmem_limit_bytes=VMEM_LIMIT_BYTES),
        name="gates",
    )(g_t)


def _mlstm_kernel(qf_ref, kf_ref, vtf_ref, abcf_ref, atf_ref, qb_ref, kb_ref, vtb_ref, abcb_ref, atb_ref,
                  hf_ref, hb_ref, st_ref, m_ref):
    L = qf_ref.shape[1]
    bf16 = jnp.bfloat16

    @pl.when(pl.program_id(1) == 0)
    def _():
        st_ref[...] = jnp.zeros_like(st_ref)
        m_ref[...] = jnp.zeros_like(m_ref)

    row = lax.broadcasted_iota(jnp.int32, (L, L), 0)
    col = lax.broadcasted_iota(jnp.int32, (L, L), 1)
    ones_rows = jnp.ones((BF16_SUBLANES, L), bf16)
    dirs = ((qf_ref, kf_ref, vtf_ref, abcf_ref, atf_ref, hf_ref),
            (qb_ref, kb_ref, vtb_ref, abcb_ref, atb_ref, hb_ref))
    pre = []
    for d, (_, _, _, abc_ref, _, _) in enumerate(dirs):
        last = L - 1 if d == 0 else 0
        a8, b8, cm8 = (abc_ref[0, kind * N_GATES + d * SUBLANES:kind * N_GATES + (d + 1) * SUBLANES, :]
                       for kind in range(3))
        m8 = m_ref[d]
        c8 = jnp.maximum(cm8, m8)
        c_last = c8[:, last:last + 1]
        m_ref[d] = b8[:, last:last + 1] + c_last
        pre.append(dict(
            c=c8, w=jnp.exp(m8 - c8), floor=jnp.exp(-b8 - c8), wk=jnp.exp(a8 - c_last).astype(bf16),
            decay=jnp.exp(m8 - c_last),
            mask=(row <= col) if d == 0 else (row >= col)))

    for h in range(M_HEADS):
        for d, (q_ref, k_ref, vt_ref, _, at_ref, ht_ref) in enumerate(dirs):
            p = pre[d]
            idx = d * M_HEADS + h
            lanes = slice(h * M_HEAD_DIM, (h + 1) * M_HEAD_DIM)
            q = q_ref[0, :, lanes]
            k = k_ref[0, :, lanes]
            vt = jnp.concatenate([vt_ref[0, lanes, :], ones_rows], axis=0)
            st = st_ref[idx]
            a_col = at_ref[0, :, d * SUBLANES + h:d * SUBLANES + h + 1]

            e = jnp.exp(jnp.where(p["mask"], a_col - p["c"][h:h + 1, :], -jnp.inf))
            pt = (_dot_nt(k, q) * e).astype(bf16)
            num = _dot(vt, pt) + p["w"][h:h + 1, :] * _dot_nt(st.astype(bf16), q)
            den = num[M_HEAD_DIM:M_HEAD_DIM + 1, :]
            ht_ref[0, lanes, :] = (num[:M_HEAD_DIM]
                                   / jnp.maximum(jnp.abs(den), p["floor"][h:h + 1, :])).astype(ht_ref.dtype)

    for h in range(M_HEADS):
        for d, (_, k_ref, vt_ref, _, _, _) in enumerate(dirs):
            p = pre[d]
            idx = d * M_HEADS + h
            lanes = slice(h * M_HEAD_DIM, (h + 1) * M_HEAD_DIM)
            vt = jnp.concatenate([vt_ref[0, lanes, :], ones_rows], axis=0)
            st_ref[idx] = (p["decay"][h:h + 1, :] * st_ref[idx]
                           + _dot(vt * p["wk"][h:h + 1, :], k_ref[0, :, lanes]))


def _mlstm(q, k, v_t, gates):
    B, T, _ = q.shape
    L = min(MLSTM_L, T)
    assert T % L == 0
    nc = T // L
    abc, a_t = gates

    def tok(width, col, rev):
        return pl.BlockSpec((1, L, width), lambda b, c: (b, nc - 1 - c if rev else c, col))

    def feat(rows, rev):
        return pl.BlockSpec((1, rows, L), lambda b, c: (b, 0, nc - 1 - c if rev else c))

    def side(rev):
        return [tok(M_WIDTH, 0, rev), tok(M_WIDTH, 0, rev), feat(M_WIDTH, rev),
                feat(3 * N_GATES, rev), tok(LANES, 0, rev)]

    return pl.pallas_call(
        _mlstm_kernel,
        grid=(B, nc),
        in_specs=side(False) + side(True),
        out_specs=[feat(M_WIDTH, False), feat(M_WIDTH, True)],
        out_shape=[
            jax.ShapeDtypeStruct((B, M_WIDTH, T), jnp.bfloat16),
            jax.ShapeDtypeStruct((B, M_WIDTH, T), jnp.bfloat16),
        ],
        scratch_shapes=[
            pltpu.VMEM((2 * M_HEADS, M_AUG, M_HEAD_DIM), jnp.float32),
            pltpu.VMEM((2, SUBLANES, 1), jnp.float32),
        ],
        compiler_params=pltpu.CompilerParams(
            dimension_semantics=("arbitrary", "arbitrary"),
            vmem_limit_bytes=VMEM_LIMIT_BYTES),
        name="mlstm",
    )(q, k, v_t, abc, a_t, q, k, v_t, abc, a_t)


_NA_KB = tuple(int(v) for v in np.clip(np.arange(GRID_W // NA_QCB) * NA_QCB - NA_KW // 2, 0, GRID_W - NA_KCB))
NA_NCB = GRID_W // NA_QCB
NA_MQ = NA_RB * NA_QCB
NA_NK = NA_WIN * NA_KCB


def _na_bias_table(rpb):
    n_dr, n_dc = 2 * NA_MAX_KH - 1, 2 * NA_KW - 1
    i = np.arange(NA_RB)[:, None]
    jr = np.arange(NA_WIN)[None, :]
    row_sel = np.zeros((3, NA_RB, NA_WIN, n_dr), np.float32)
    for rt in range(3):
        off = (0, -NA_RB // 2, -NA_RB)[rt]
        if rt == 0:
            rs = np.maximum(i - NA_MAX_KH // 2, 0)
        elif rt == 1:
            rs = i - NA_MAX_KH // 2
        else:
            rs = np.minimum(i - NA_MAX_KH // 2, 0)
        krow = off + jr
        ii, jj = np.nonzero((krow >= rs) & (krow < rs + NA_MAX_KH))
        row_sel[rt, ii, jj, (krow - i)[ii, jj] + NA_MAX_KH - 1] = 1.0
    qc = np.arange(NA_QCB)[:, None]
    kc = np.arange(NA_KCB)[None, :]
    col_sel = np.zeros((NA_NCB, NA_QCB, NA_KCB, n_dc), np.float32)
    for n in range(NA_NCB):
        qcol = n * NA_QCB + qc
        kcol = _NA_KB[n] + kc
        cs = np.clip(qcol - NA_KW // 2, 0, GRID_W - NA_KW)
        qq, kk = np.nonzero((kcol >= cs) & (kcol < cs + NA_KW))
        col_sel[n, qq, kk, (kcol - qcol)[qq, kk] + NA_KW - 1] = 1.0
    ok = (np.einsum("xijr->xij", row_sel)[:, None, :, None, :, None]
          * np.einsum("nqkc->nqk", col_sel)[None, :, None, :, None, :]) > 0
    hi = lax.Precision.HIGHEST
    r = rpb.astype(jnp.float32).reshape(NA_HEADS // NA_HPB, NA_HPB, n_dr, n_dc)
    a = jnp.einsum("ghrc,xijr->gxhijc", r, row_sel, precision=hi)
    t = jnp.einsum("gxhijc,nqkc->gxnhiqjk", a, col_sel, precision=hi)
    t = jnp.where(ok[None, :, :, None], t * LOG2E, NEG)
    return t.reshape(NA_HEADS // NA_HPB, 3 * NA_NCB, NA_HPB * NA_MQ, NA_NK)


def _natten_kernel(q_ref, *refs):
    kv_refs, (z_ref, tab_ref, o_ref, acc_s) = refs[:-4], refs[-4:]
    nrb = pl.num_programs(2) * NA_BPS
    tq = NA_RB * GRID_W
    lane = lax.broadcasted_iota(jnp.int32, (NA_MQ, NA_LANES), 1) // NA_HEAD_DIM

    def keys(xf, n):
        kb = _NA_KB[n]
        return jnp.concatenate([xf[j * GRID_W + kb:j * GRID_W + kb + NA_KCB] for j in range(NA_WIN)],
                               axis=0).astype(jnp.bfloat16)

    for blk in range(NA_BPS):
        rb = pl.program_id(2) * NA_BPS + blk
        rt = jnp.where(rb == 0, 0, jnp.where(rb == nrb - 1, 2, 1))
        kf = kv_refs[2 * blk][0].astype(jnp.float32)
        vf = kv_refs[2 * blk + 1][0].astype(jnp.float32)
        q = q_ref[0, blk * tq:(blk + 1) * tq, :]
        for n in range(NA_NCB):
            qn = jnp.concatenate([q[i * GRID_W + n * NA_QCB:i * GRID_W + (n + 1) * NA_QCB] for i in range(NA_RB)],
                                 axis=0)
            qs = jnp.concatenate([jnp.where(lane == hh, qn, 0.0).astype(jnp.bfloat16) for hh in range(NA_HPB)],
                                 axis=0)
            s = _dot_nt(qs, keys(kf, n)) + tab_ref[0, rt * NA_NCB + n]
            m = jnp.max(s, axis=1, keepdims=True)
            e = jnp.exp2(s - m)
            l = jnp.sum(e, axis=1, keepdims=True)
            o = _dot(e.astype(jnp.bfloat16), keys(vf, n)) / l
            on = jnp.zeros((NA_MQ, NA_LANES), jnp.float32)
            for hh in range(NA_HPB):
                on = jnp.where(lane == hh, o[hh * NA_MQ:(hh + 1) * NA_MQ], on)
            for i in range(NA_RB):
                r0 = blk * tq + i * GRID_W + n * NA_QCB
                acc_s[r0:r0 + NA_QCB, :] = on[i * NA_QCB:(i + 1) * NA_QCB]
    o_ref[0] = (acc_s[...] * _silu(z_ref[0].astype(jnp.float32))).astype(o_ref.dtype)


def _natten(proj, table):
    B, T, _ = proj.shape
    rows = T // GRID_W
    assert T % GRID_W == 0 and rows % (NA_RB * NA_BPS) == 0 and rows >= NA_WIN
    ng = NA_HEADS // NA_HPB
    tq = NA_BPS * NA_RB * GRID_W
    cpt = D_MODEL // NA_LANES

    def win(col, blk):
        def index(g, b, s):
            start = jnp.clip((s * NA_BPS + blk) * NA_RB - NA_MAX_KH // 2, 0, rows - NA_WIN)
            return (b, start * GRID_W, (col * cpt + g) * NA_LANES)
        return pl.BlockSpec((pl.Element(1), pl.Element(NA_WIN * GRID_W), pl.Element(NA_LANES)), index)

    windows = []
    for blk in range(NA_BPS):
        windows += [win(COL_NK, blk), win(COL_NV, blk)]
    return pl.pallas_call(
        _natten_kernel,
        grid=(ng, B, rows // (NA_RB * NA_BPS)),
        in_specs=[pl.BlockSpec((1, tq, NA_LANES), lambda g, b, s: (b, s, COL_NQ * cpt + g))]
        + windows
        + [pl.BlockSpec((1, tq, NA_LANES), lambda g, b, s: (b, s, COL_NZ * cpt + g)),
           pl.BlockSpec((1, 3 * NA_NCB, NA_HPB * NA_MQ, NA_NK), lambda g, b, s: (g, 0, 0, 0))],
        out_specs=pl.BlockSpec((1, tq, NA_LANES), lambda g, b, s: (b, s, g)),
        out_shape=jax.ShapeDtypeStruct((B, T, NA_WIDTH), jnp.bfloat16),
        scratch_shapes=[pltpu.VMEM((tq, NA_LANES), jnp.float32)],
        compiler_params=pltpu.CompilerParams(
            dimension_semantics=("arbitrary", "arbitrary", "arbitrary"),
            vmem_limit_bytes=VMEM_LIMIT_BYTES),
        name="natten",
    )(proj, *([proj] * (2 * NA_BPS)), proj, table)


def _out_kernel(x_ref, hf_ref, hb_ref, o_ref, z_ref, yb_ref, ga_ref, gb_ref, mhw_ref, wda_ref, wdb_ref, wo_ref,
                fnw_ref, y_ref):
    f32 = jnp.float32
    tm = x_ref.shape[1]
    for r0 in range(0, tm, OUT_SUB):
        rows = slice(r0, r0 + OUT_SUB)
        h = hf_ref[0, :, rows].astype(f32) + hb_ref[0, :, rows].astype(f32)
        h = h.T * _sigmoid(o_ref[0, rows, :].astype(f32))
        parts = []
        for hd in range(M_HEADS):
            hh = h[:, hd * M_HEAD_DIM:(hd + 1) * M_HEAD_DIM]
            parts.append(hh * lax.rsqrt(jnp.mean(hh * hh, axis=-1, keepdims=True) + EPS))
        hn = jnp.concatenate(parts, axis=-1) * mhw_ref[...]
        ya = (hn * _silu(z_ref[0, rows, :].astype(f32))).astype(jnp.bfloat16)
        merged = (_sigmoid(ga_ref[0, rows, :].astype(f32)) * _dot(ya, wda_ref[...])
                  + _sigmoid(gb_ref[0, rows, :].astype(f32)) * _dot(yb_ref[0, rows, :], wdb_ref[...]))
        y = x_ref[0, rows, :] + _dot(merged.astype(jnp.bfloat16), wo_ref[...])
        y = y * lax.rsqrt(jnp.mean(y * y, axis=-1, keepdims=True) + EPS)
        y_ref[0, rows, :] = y * fnw_ref[...]


def _out(x, h_f, h_b, proj, y_b, mh_w, w_da, w_db, w_o, fn_w):
    B, T, D = x.shape
    tm = min(OUT_TM, T)
    assert T % tm == 0

    def tok(col=0):
        return pl.BlockSpec((1, tm, D_MODEL), lambda b, i, col=col: (b, i, col))

    def feat():
        return pl.BlockSpec((1, M_WIDTH, tm), lambda b, i: (b, 0, i))

    def const(shape):
        return pl.BlockSpec(shape, lambda b, i: (0,) * len(shape))

    return pl.pallas_call(
        _out_kernel,
        grid=(B, T // tm),
        in_specs=[tok(), feat(), feat(), tok(COL_MO), tok(COL_MZ), tok(), tok(COL_GA), tok(COL_GB),
                  const((1, D)), const((D, D)), const((D, D)), const((D, D)), const((1, D))],
        out_specs=tok(),
        out_shape=jax.ShapeDtypeStruct((B, T, D), jnp.float32),
        compiler_params=pltpu.CompilerParams(
            dimension_semantics=("arbitrary", "arbitrary"),
            vmem_limit_bytes=VMEM_LIMIT_BYTES),
        name="out",
    )(x, h_f, h_b, proj, proj, y_b, proj, proj, mh_w, w_da, w_db, w_o, fn_w)


def _prepare(norm_w, w_in, b_gate, conv_w, conv_b, mh_norm_w, rpb, w_down_a, w_down_b, w_out, final_norm_w):
    bf16 = jnp.bfloat16
    g0 = 5 * M_WIDTH
    v0 = 2 * M_WIDTH
    nq0 = g0 + N_GATES
    w_steps = jnp.concatenate([w_in[:, v0:v0 + M_WIDTH].T, w_in[:, :v0], w_in[:, v0 + M_WIDTH:g0],
                               w_in[:, nq0:nq0 + NA_WIDTH] * (LOG2E * NA_HEAD_DIM ** -0.5),
                               w_in[:, nq0 + NA_WIDTH:]], axis=1).astype(bf16)
    w_gt = w_in[:, g0:g0 + N_GATES].T.astype(bf16)
    cw = jnp.pad(conv_w, ((0, SUBLANES - CONV_W), (0, 0)))
    return dict(
        norm_w=norm_w.reshape(1, D_MODEL), w_steps=w_steps, w_gt=w_gt, b_gt=b_gate.reshape(N_GATES, 1),
        conv_w=cw, conv_b=conv_b.reshape(1, 2 * M_WIDTH), mh_w=mh_norm_w.reshape(1, M_WIDTH),
        table=_na_bias_table(rpb), w_da=w_down_a.astype(bf16), w_db=w_down_b.astype(bf16),
        w_o=w_out.astype(bf16), fn_w=final_norm_w.reshape(1, D_MODEL))


def _trunk(x, p):
    proj, q, k, v_t, g_t = _proj(x, p["norm_w"], p["w_steps"], p["w_gt"], p["b_gt"], p["conv_w"], p["conv_b"])
    h_f, h_b = _mlstm(q, k, v_t, _gates(g_t, min(MLSTM_L, x.shape[1])))
    y_b = _natten(proj, p["table"])
    return _out(x, h_f, h_b, proj, y_b, p["mh_w"], p["w_da"], p["w_db"], p["w_o"], p["fn_w"])


def kernel(x_prompt, x_sample, norm_w, w_in, b_gate, conv_w, conv_b, mh_norm_w, rpb, w_down_a, w_down_b, w_out,
           final_norm_w):
    assert norm_w.shape[0] == 1, "single-layer trunk"
    p = _prepare(norm_w[0], w_in[0], b_gate[0], conv_w[0], conv_b[0], mh_norm_w[0], rpb[0],
                 w_down_a[0], w_down_b[0], w_out[0], final_norm_w)
    return (_trunk(x_prompt, p), _trunk(x_sample, p))
```

```python
import functools

import numpy as np
import jax
import jax.numpy as jnp
from jax import lax
from jax.experimental import pallas as pl
from jax.experimental.pallas import tpu as pltpu

D_MODEL = 1024
GRID_W = 64
M_HEADS = 4
M_HEAD_DIM = 256
M_WIDTH = M_HEADS * M_HEAD_DIM
CONV_W = 5
N_GATES = 4 * M_HEADS
NA_HEADS = 16
NA_HEAD_DIM = 64
NA_WIDTH = NA_HEADS * NA_HEAD_DIM
NA_MAX_KH = 8
NA_KW = 16
NA_QCB = 16
NA_KCB = 32
EPS = 1e-6
NEG = -1e30
LOG2E = 1.4426950408889634

LANES = 128
SUBLANES = 8
BF16_SUBLANES = 16
VMEM_LIMIT_BYTES = 56 * 1024 * 1024

STEP_MV, STEP_MQ, STEP_MK = 0, 1, 2
N_STEPS = 11
COL_MO, COL_MZ, COL_NQ, COL_NK, COL_NV, COL_NZ, COL_GA, COL_GB = range(8)
D_MAIN = 8 * D_MODEL

PROJ_TM = 1024
HALO = BF16_SUBLANES
CONV_PIECE = 256
MLSTM_L = 256
GATES_TT = 2048
M_AUG = M_HEAD_DIM + BF16_SUBLANES
NA_RB = 4
NA_WIN = NA_RB + NA_MAX_KH
NA_BPS = 4
NA_HPB = 4
NA_LANES = NA_HPB * NA_HEAD_DIM
OUT_TM = 512
OUT_SUB = 256


def _dot(a, b):
    return jnp.dot(a, b, preferred_element_type=jnp.float32)


def _dot_nt(a, b):
    return lax.dot_general(a, b, (((1,), (1,)), ((), ())), preferred_element_type=jnp.float32)


def _sigmoid(x):
    return 1.0 / (1.0 + jnp.exp(-x))


def _silu(x):
    return x / (1.0 + jnp.exp(-x))


def _log_sigmoid(x):
    return jnp.minimum(x, 0.0) - jnp.log1p(jnp.exp(-jnp.abs(x)))


def _proj_kernel(x_ref, xp_ref, xn_ref_, nw_ref, w_ref, wgt_ref, bgt_ref, cw_ref, cb_ref,
                 o_ref, q_ref, k_ref, vt_ref, gt_ref, xn_s, acc_s):
    i = pl.program_id(1)
    j = pl.program_id(2)
    ni = pl.num_programs(1)
    tm = x_ref.shape[1]
    pad = CONV_W // 2

    def norm(x):
        y = x * lax.rsqrt(jnp.mean(x * x, axis=-1, keepdims=True) + EPS)
        return y * nw_ref[...]

    def conv_tile(out_ref, scale):
        for c in range(tm // CONV_PIECE):
            r0 = c * CONV_PIECE
            acc_s[c % 2] = _dot(xn_s[r0:r0 + CONV_PIECE + 2 * HALO, :], w_ref[...])
            y = cb_ref[...]
            for t in range(CONV_W):
                y = y + acc_s[c % 2, pl.ds(HALO - pad + t, CONV_PIECE), :] * cw_ref[t:t + 1, :]
            y = _silu(y).astype(out_ref.dtype)
            out_ref[0, r0:r0 + CONV_PIECE, :] = y if scale is None else y * scale

    @pl.when(j == STEP_MV)
    def _():
        cur = norm(x_ref[0]).astype(jnp.bfloat16)
        prev = jnp.where(i > 0, norm(xp_ref[0]), 0.0).astype(jnp.bfloat16)
        nxt = jnp.where(i < ni - 1, norm(xn_ref_[0]), 0.0).astype(jnp.bfloat16)
        xn_s[0:HALO, :] = prev
        xn_s[HALO:HALO + tm, :] = cur
        xn_s[HALO + tm:2 * HALO + tm, :] = nxt
        gt_ref[0] = _dot_nt(wgt_ref[...], cur) + bgt_ref[...]
        vt_ref[0] = _dot_nt(w_ref[...], cur).astype(vt_ref.dtype)

    @pl.when(j == STEP_MQ)
    def _():
        conv_tile(q_ref, None)

    @pl.when(j == STEP_MK)
    def _():
        conv_tile(k_ref, M_HEAD_DIM ** -0.5)

    @pl.when(j > STEP_MK)
    def _():
        o_ref[0] = _dot(xn_s[HALO:HALO + tm, :], w_ref[...]).astype(o_ref.dtype)


def _proj(x, norm_w, w_steps, w_gt, b_gt, conv_w, conv_b):
    B, T, D = x.shape
    tm = min(PROJ_TM, T)
    assert T % tm == 0 and tm % CONV_PIECE == 0
    ni = T // tm
    hb = tm // HALO
    tile = pl.BlockSpec((1, tm, D_MODEL), lambda b, i, j: (b, i, 0))
    return pl.pallas_call(
        _proj_kernel,
        grid=(B, ni, N_STEPS),
        in_specs=[
            pl.BlockSpec((1, tm, D), lambda b, i, j: (b, i, 0)),
            pl.BlockSpec((1, HALO, D), lambda b, i, j: (b, jnp.maximum(i * hb - 1, 0), 0)),
            pl.BlockSpec((1, HALO, D), lambda b, i, j: (b, jnp.minimum((i + 1) * hb, T // HALO - 1), 0)),
            pl.BlockSpec((1, D), lambda b, i, j: (0, 0)),
            pl.BlockSpec((D, D_MODEL), lambda b, i, j: (0, j)),
            pl.BlockSpec((N_GATES, D), lambda b, i, j: (0, 0)),
            pl.BlockSpec((N_GATES, 1), lambda b, i, j: (0, 0)),
            pl.BlockSpec((SUBLANES, D_MODEL), lambda b, i, j: (0, jnp.clip(j - STEP_MQ, 0, 1))),
            pl.BlockSpec((1, D_MODEL), lambda b, i, j: (0, jnp.clip(j - STEP_MQ, 0, 1))),
        ],
        out_specs=[
            pl.BlockSpec((1, tm, D_MODEL), lambda b, i, j: (b, i, jnp.maximum(j - STEP_MK - 1, 0))),
            tile, tile,
            pl.BlockSpec((1, M_WIDTH, tm), lambda b, i, j: (b, 0, i)),
            pl.BlockSpec((1, N_GATES, tm), lambda b, i, j: (b, 0, i)),
        ],
        out_shape=[
            jax.ShapeDtypeStruct((B, T, D_MAIN), jnp.bfloat16),
            jax.ShapeDtypeStruct((B, T, M_WIDTH), jnp.bfloat16),
            jax.ShapeDtypeStruct((B, T, M_WIDTH), jnp.bfloat16),
            jax.ShapeDtypeStruct((B, M_WIDTH, T), jnp.bfloat16),
            jax.ShapeDtypeStruct((B, N_GATES, T), jnp.float32),
        ],
        scratch_shapes=[
            pltpu.VMEM((tm + 2 * HALO, D), jnp.bfloat16),
            pltpu.VMEM((2, CONV_PIECE + 2 * HALO, D_MODEL), jnp.float32),
        ],
        compiler_params=pltpu.CompilerParams(
            dimension_semantics=("arbitrary", "arbitrary", "arbitrary"),
            vmem_limit_bytes=VMEM_LIMIT_BYTES),
        name="proj",
    )(x, x, x, norm_w, w_steps, w_gt, b_gt, conv_w, conv_b)


def _scan_lanes(x, pos, L, d, op, fill):
    n = x.shape[1]
    sh = 1
    while sh < L:
        if d == 0:
            x = op(x, jnp.where(pos >= sh, pltpu.roll(x, sh, axis=1), fill))
        else:
            x = op(x, jnp.where(pos < L - sh, pltpu.roll(x, n - sh, axis=1), fill))
        sh *= 2
    return x


def _gates_kernel(L, gt_ref, abc_ref, at_ref):
    n = gt_ref.shape[2]
    pos = lax.broadcasted_iota(jnp.int32, (SUBLANES, n), 1) % L
    a_rows = []
    for d in range(2):
        rows = slice(d * SUBLANES, (d + 1) * SUBLANES)
        g8 = gt_ref[0, rows, :]
        b8 = _scan_lanes(_log_sigmoid(g8), pos, L, d, jnp.add, 0.0)
        b8 = pltpu.roll(b8, M_HEADS, axis=0)
        a8 = g8 - b8
        abc_ref[0, d * SUBLANES:(d + 1) * SUBLANES, :] = a8
        abc_ref[0, N_GATES + d * SUBLANES:N_GATES + (d + 1) * SUBLANES, :] = b8
        abc_ref[0, 2 * N_GATES + d * SUBLANES:2 * N_GATES + (d + 1) * SUBLANES, :] = _scan_lanes(
            a8, pos, L, d, jnp.maximum, -jnp.inf)
        a_rows.append(a8)
    pad = jnp.zeros((LANES - 2 * SUBLANES, n), jnp.float32)
    at_ref[0] = jnp.concatenate(a_rows + [pad], axis=0).T


def _gates(g_t, L):
    B, _, T = g_t.shape
    tg = min(GATES_TT, T)
    assert T % tg == 0 and tg % L == 0
    return pl.pallas_call(
        functools.partial(_gates_kernel, L),
        grid=(B, T // tg),
        in_specs=[pl.BlockSpec((1, N_GATES, tg), lambda b, i: (b, 0, i))],
        out_specs=[pl.BlockSpec((1, 3 * N_GATES, tg), lambda b, i: (b, 0, i)),
                   pl.BlockSpec((1, tg, LANES), lambda b, i: (b, i, 0))],
        out_shape=[jax.ShapeDtypeStruct((B, 3 * N_GATES, T), jnp.float32),
                   jax.ShapeDtypeStruct((B, T, LANES), jnp.float32)],
        compiler_params=pltpu.CompilerParams(
            dimension_semantics=("arbitrary", "arbitrary"),
            vmem_limit_bytes=VMEM_LIMIT_BYTES),
        name="gates",
    )(g_t)


def _mlstm_kernel(qf_ref, kf_ref, vtf_ref, abcf_ref, atf_ref, qb_ref, kb_ref, vtb_ref, abcb_ref, atb_ref,
                  hf_ref, hb_ref, st_ref, m_ref):
    L = qf_ref.shape[1]
    bf16 = jnp.bfloat16

    @pl.when(pl.program_id(1) == 0)
    def _():
        st_ref[...] = jnp.zeros_like(st_ref)
        m_ref[...] = jnp.zeros_like(m_ref)

    row = lax.broadcasted_iota(jnp.int32, (L, L), 0)
    col = lax.broadcasted_iota(jnp.int32, (L, L), 1)
    ones_rows = jnp.ones((BF16_SUBLANES, L), bf16)
    dirs = ((qf_ref, kf_ref, vtf_ref, abcf_ref, atf_ref, hf_ref),
            (qb_ref, kb_ref, vtb_ref, abcb_ref, atb_ref, hb_ref))
    pre = []
    for d, (_, _, _, abc_ref, _, _) in enumerate(dirs):
        last = L - 1 if d == 0 else 0
        a8, b8, cm8 = (abc_ref[0, kind * N_GATES + d * SUBLANES:kind * N_GATES + (d + 1) * SUBLANES, :]
                       for kind in range(3))
        m8 = m_ref[d]
        c8 = jnp.maximum(cm8, m8)
        c_last = c8[:, last:last + 1]
        m_ref[d] = b8[:, last:last + 1] + c_last
        pre.append(dict(
            c=c8, w=jnp.exp(m8 - c8), floor=jnp.exp(-b8 - c8), wk=jnp.exp(a8 - c_last).astype(bf16),
            decay=jnp.exp(m8 - c_last),
            mask=(row <= col) if d == 0 else (row >= col)))

    for h in range(M_HEADS):
        for d, (q_ref, k_ref, vt_ref, _, at_ref, ht_ref) in enumerate(dirs):
            p = pre[d]
            idx = d * M_HEADS + h
            lanes = slice(h * M_HEAD_DIM, (h + 1) * M_HEAD_DIM)
            q = q_ref[0, :, lanes]
            k = k_ref[0, :, lanes]
            vt = jnp.concatenate([vt_ref[0, lanes, :], ones_rows], axis=0)
            st = st_ref[idx]
            a_col = at_ref[0, :, d * SUBLANES + h:d * SUBLANES + h + 1]

            e = jnp.exp(jnp.where(p["mask"], a_col - p["c"][h:h + 1, :], -jnp.inf))
            pt = (_dot_nt(k, q) * e).astype(bf16)
            num = _dot(vt, pt) + p["w"][h:h + 1, :] * _dot_nt(st.astype(bf16), q)
            den = num[M_HEAD_DIM:M_HEAD_DIM + 1, :]
            ht_ref[0, lanes, :] = (num[:M_HEAD_DIM]
                                   / jnp.maximum(jnp.abs(den), p["floor"][h:h + 1, :])).astype(ht_ref.dtype)

    for h in range(M_HEADS):
        for d, (_, k_ref, vt_ref, _, _, _) in enumerate(dirs):
            p = pre[d]
            idx = d * M_HEADS + h
            lanes = slice(h * M_HEAD_DIM, (h + 1) * M_HEAD_DIM)
            vt = jnp.concatenate([vt_ref[0, lanes, :], ones_rows], axis=0)
            st_ref[idx] = (p["decay"][h:h + 1, :] * st_ref[idx]
                           + _dot(vt * p["wk"][h:h + 1, :], k_ref[0, :, lanes]))


def _mlstm(q, k, v_t, gates):
    B, T, _ = q.shape
    L = min(MLSTM_L, T)
    assert T % L == 0
    nc = T // L
    abc, a_t = gates

    def tok(width, col, rev):
        return pl.BlockSpec((1, L, width), lambda b, c: (b, nc - 1 - c if rev else c, col))

    def feat(rows, rev):
        return pl.BlockSpec((1, rows, L), lambda b, c: (b, 0, nc - 1 - c if rev else c))

    def side(rev):
        return [tok(M_WIDTH, 0, rev), tok(M_WIDTH, 0, rev), feat(M_WIDTH, rev),
                feat(3 * N_GATES, rev), tok(LANES, 0, rev)]

    return pl.pallas_call(
        _mlstm_kernel,
        grid=(B, nc),
        in_specs=side(False) + side(True),
        out_specs=[feat(M_WIDTH, False), feat(M_WIDTH, True)],
        out_shape=[
            jax.ShapeDtypeStruct((B, M_WIDTH, T), jnp.bfloat16),
            jax.ShapeDtypeStruct((B, M_WIDTH, T), jnp.bfloat16),
        ],
        scratch_shapes=[
            pltpu.VMEM((2 * M_HEADS, M_AUG, M_HEAD_DIM), jnp.float32),
            pltpu.VMEM((2, SUBLANES, 1), jnp.float32),
        ],
        compiler_params=pltpu.CompilerParams(
            dimension_semantics=("arbitrary", "arbitrary"),
            vmem_limit_bytes=VMEM_LIMIT_BYTES),
        name="mlstm",
    )(q, k, v_t, abc, a_t, q, k, v_t, abc, a_t)


_NA_KB = tuple(int(v) for v in np.clip(np.arange(GRID_W // NA_QCB) * NA_QCB - NA_KW // 2, 0, GRID_W - NA_KCB))
NA_NCB = GRID_W // NA_QCB
NA_MQ = NA_RB * NA_QCB
NA_NK = NA_WIN * NA_KCB


def _na_bias_table(rpb):
    n_dr, n_dc = 2 * NA_MAX_KH - 1, 2 * NA_KW - 1
    i = np.arange(NA_RB)[:, None]
    jr = np.arange(NA_WIN)[None, :]
    row_sel = np.zeros((3, NA_RB, NA_WIN, n_dr), np.float32)
    for rt in range(3):
        off = (0, -NA_MAX_KH // 2, NA_RB - NA_WIN)[rt]
        if rt == 0:
            rs = np.maximum(i - NA_MAX_KH // 2, 0)
        elif rt == 1:
            rs = i - NA_MAX_KH // 2
        else:
            rs = np.minimum(i - NA_MAX_KH // 2, NA_RB - NA_MAX_KH)
        krow = off + jr
        ii, jj = np.nonzero((krow >= rs) & (krow < rs + NA_MAX_KH))
        row_sel[rt, ii, jj, (krow - i)[ii, jj] + NA_MAX_KH - 1] = 1.0
    qc = np.arange(NA_QCB)[:, None]
    kc = np.arange(NA_KCB)[None, :]
    col_sel = np.zeros((NA_NCB, NA_QCB, NA_KCB, n_dc), np.float32)
    for n in range(NA_NCB):
        qcol = n * NA_QCB + qc
        kcol = _NA_KB[n] + kc
        cs = np.clip(qcol - NA_KW // 2, 0, GRID_W - NA_KW)
        qq, kk = np.nonzero((kcol >= cs) & (kcol < cs + NA_KW))
        col_sel[n, qq, kk, (kcol - qcol)[qq, kk] + NA_KW - 1] = 1.0
    ok = (np.einsum("xijr->xij", row_sel)[:, None, :, None, :, None]
          * np.einsum("nqkc->nqk", col_sel)[None, :, None, :, None, :]) > 0
    hi = lax.Precision.HIGHEST
    r = rpb.astype(jnp.float32).reshape(NA_HEADS // NA_HPB, NA_HPB, n_dr, n_dc)
    a = jnp.einsum("ghrc,xijr->gxhijc", r, row_sel, precision=hi)
    t = jnp.einsum("gxhijc,nqkc->gxnhiqjk", a, col_sel, precision=hi)
    t = jnp.where(ok[None, :, :, None], t * LOG2E, NEG)
    return t.reshape(NA_HEADS // NA_HPB, 3 * NA_NCB, NA_HPB * NA_MQ, NA_NK)


def _natten_kernel(q_ref, *refs):
    kv_refs, (z_ref, tab_ref, o_ref, acc_s) = refs[:-4], refs[-4:]
    nrb = pl.num_programs(2) * NA_BPS
    tq = NA_RB * GRID_W
    lane = lax.broadcasted_iota(jnp.int32, (NA_MQ, NA_LANES), 1) // NA_HEAD_DIM

    def keys(xf, n):
        kb = _NA_KB[n]
        return jnp.concatenate([xf[j * GRID_W + kb:j * GRID_W + kb + NA_KCB] for j in range(NA_WIN)],
                               axis=0).astype(jnp.bfloat16)

    for blk in range(NA_BPS):
        rb = pl.program_id(2) * NA_BPS + blk
        rt = jnp.where(rb == 0, 0, jnp.where(rb == nrb - 1, 2, 1))
        kf = kv_refs[2 * blk][0].astype(jnp.float32)
        vf = kv_refs[2 * blk + 1][0].astype(jnp.float32)
        q = q_ref[0, blk * tq:(blk + 1) * tq, :]
        for n in range(NA_NCB):
            qn = jnp.concatenate([q[i * GRID_W + n * NA_QCB:i * GRID_W + (n + 1) * NA_QCB] for i in range(NA_RB)],
                                 axis=0)
            qs = jnp.concatenate([jnp.where(lane == hh, qn, 0.0).astype(jnp.bfloat16) for hh in range(NA_HPB)],
                                 axis=0)
            s = _dot_nt(qs, keys(kf, n)) + tab_ref[0, rt * NA_NCB + n]
            m = jnp.max(s, axis=1, keepdims=True)
            e = jnp.exp2(s - m)
            l = jnp.sum(e, axis=1, keepdims=True)
            o = _dot(e.astype(jnp.bfloat16), keys(vf, n)) / l
            on = jnp.zeros((NA_MQ, NA_LANES), jnp.float32)
            for hh in range(NA_HPB):
                on = jnp.where(lane == hh, o[hh * NA_MQ:(hh + 1) * NA_MQ], on)
            for i in range(NA_RB):
                r0 = blk * tq + i * GRID_W + n * NA_QCB
                acc_s[r0:r0 + NA_QCB, :] = on[i * NA_QCB:(i + 1) * NA_QCB]
    o_ref[0] = (acc_s[...] * _silu(z_ref[0].astype(jnp.float32))).astype(o_ref.dtype)


def _natten(proj, table):
    B, T, _ = proj.shape
    rows = T // GRID_W
    assert T % GRID_W == 0 and rows % (NA_RB * NA_BPS) == 0 and rows >= NA_WIN
    ng = NA_HEADS // NA_HPB
    tq = NA_BPS * NA_RB * GRID_W
    cpt = D_MODEL // NA_LANES

    def win(col, blk):
        def index(g, b, s):
            start = jnp.clip((s * NA_BPS + blk) * NA_RB - NA_MAX_KH // 2, 0, rows - NA_WIN)
            return (b, start * GRID_W, (col * cpt + g) * NA_LANES)
        return pl.BlockSpec((pl.Element(1), pl.Element(NA_WIN * GRID_W), pl.Element(NA_LANES)), index)

    windows = []
    for blk in range(NA_BPS):
        windows += [win(COL_NK, blk), win(COL_NV, blk)]
    return pl.pallas_call(
        _natten_kernel,
        grid=(ng, B, rows // (NA_RB * NA_BPS)),
        in_specs=[pl.BlockSpec((1, tq, NA_LANES), lambda g, b, s: (b, s, COL_NQ * cpt + g))]
        + windows
        + [pl.BlockSpec((1, tq, NA_LANES), lambda g, b, s: (b, s, COL_NZ * cpt + g)),
           pl.BlockSpec((1, 3 * NA_NCB, NA_HPB * NA_MQ, NA_NK), lambda g, b, s: (g, 0, 0, 0))],
        out_specs=pl.BlockSpec((1, tq, NA_LANES), lambda g, b, s: (b, s, g)),
        out_shape=jax.ShapeDtypeStruct((B, T, NA_WIDTH), jnp.bfloat16),
        scratch_shapes=[pltpu.VMEM((tq, NA_LANES), jnp.float32)],
        compiler_params=pltpu.CompilerParams(
            dimension_semantics=("arbitrary", "arbitrary", "arbitrary"),
            vmem_limit_bytes=VMEM_LIMIT_BYTES),
        name="natten",
    )(proj, *([proj] * (2 * NA_BPS)), proj, table)


def _out_kernel(x_ref, hf_ref, hb_ref, o_ref, z_ref, yb_ref, ga_ref, gb_ref, mhw_ref, wda_ref, wdb_ref, wo_ref,
                fnw_ref, y_ref):
    f32 = jnp.float32
    tm = x_ref.shape[1]
    for r0 in range(0, tm, OUT_SUB):
        rows = slice(r0, r0 + OUT_SUB)
        h = hf_ref[0, :, rows].astype(f32) + hb_ref[0, :, rows].astype(f32)
        h = h.T * _sigmoid(o_ref[0, rows, :].astype(f32))
        parts = []
        for hd in range(M_HEADS):
            hh = h[:, hd * M_HEAD_DIM:(hd + 1) * M_HEAD_DIM]
            parts.append(hh * lax.rsqrt(jnp.mean(hh * hh, axis=-1, keepdims=True) + EPS))
        hn = jnp.concatenate(parts, axis=-1) * mhw_ref[...]
        ya = (hn * _silu(z_ref[0, rows, :].astype(f32))).astype(jnp.bfloat16)
        merged = (_sigmoid(ga_ref[0, rows, :].astype(f32)) * _dot(ya, wda_ref[...])
                  + _sigmoid(gb_ref[0, rows, :].astype(f32)) * _dot(yb_ref[0, rows, :], wdb_ref[...]))
        y = x_ref[0, rows, :] + _dot(merged.astype(jnp.bfloat16), wo_ref[...])
        y = y * lax.rsqrt(jnp.mean(y * y, axis=-1, keepdims=True) + EPS)
        y_ref[0, rows, :] = y * fnw_ref[...]


def _out(x, h_f, h_b, proj, y_b, mh_w, w_da, w_db, w_o, fn_w):
    B, T, D = x.shape
    tm = min(OUT_TM, T)
    assert T % tm == 0

    def tok(col=0):
        return pl.BlockSpec((1, tm, D_MODEL), lambda b, i, col=col: (b, i, col))

    def feat():
        return pl.BlockSpec((1, M_WIDTH, tm), lambda b, i: (b, 0, i))

    def const(shape):
        return pl.BlockSpec(shape, lambda b, i: (0,) * len(shape))

    return pl.pallas_call(
        _out_kernel,
        grid=(B, T // tm),
        in_specs=[tok(), feat(), feat(), tok(COL_MO), tok(COL_MZ), tok(), tok(COL_GA), tok(COL_GB),
                  const((1, D)), const((D, D)), const((D, D)), const((D, D)), const((1, D))],
        out_specs=tok(),
        out_shape=jax.ShapeDtypeStruct((B, T, D), jnp.float32),
        compiler_params=pltpu.CompilerParams(
            dimension_semantics=("arbitrary", "arbitrary"),
            vmem_limit_bytes=VMEM_LIMIT_BYTES),
        name="out",
    )(x, h_f, h_b, proj, proj, y_b, proj, proj, mh_w, w_da, w_db, w_o, fn_w)


def _prepare(norm_w, w_in, b_gate, conv_w, conv_b, mh_norm_w, rpb, w_down_a, w_down_b, w_out, final_norm_w):
    bf16 = jnp.bfloat16
    g0 = 5 * M_WIDTH
    v0 = 2 * M_WIDTH
    nq0 = g0 + N_GATES
    w_steps = jnp.concatenate([w_in[:, v0:v0 + M_WIDTH].T, w_in[:, :v0], w_in[:, v0 + M_WIDTH:g0],
                               w_in[:, nq0:nq0 + NA_WIDTH] * (LOG2E * NA_HEAD_DIM ** -0.5),
                               w_in[:, nq0 + NA_WIDTH:]], axis=1).astype(bf16)
    w_gt = w_in[:, g0:g0 + N_GATES].T.astype(bf16)
    cw = jnp.pad(conv_w, ((0, SUBLANES - CONV_W), (0, 0)))
    return dict(
        norm_w=norm_w.reshape(1, D_MODEL), w_steps=w_steps, w_gt=w_gt, b_gt=b_gate.reshape(N_GATES, 1),
        conv_w=cw, conv_b=conv_b.reshape(1, 2 * M_WIDTH), mh_w=mh_norm_w.reshape(1, M_WIDTH),
        table=_na_bias_table(rpb), w_da=w_down_a.astype(bf16), w_db=w_down_b.astype(bf16),
        w_o=w_out.astype(bf16), fn_w=final_norm_w.reshape(1, D_MODEL))


def _trunk(x, p):
    proj, q, k, v_t, g_t = _proj(x, p["norm_w"], p["w_steps"], p["w_gt"], p["b_gt"], p["conv_w"], p["conv_b"])
    h_f, h_b = _mlstm(q, k, v_t, _gates(g_t, min(MLSTM_L, x.shape[1])))
    y_b = _natten(proj, p["table"])
    return _out(x, h_f, h_b, proj, y_b, p["mh_w"], p["w_da"], p["w_db"], p["w_o"], p["fn_w"])


def kernel(x_prompt, x_sample, norm_w, w_in, b_gate, conv_w, conv_b, mh_norm_w, rpb, w_down_a, w_down_b, w_out,
           final_norm_w):
    assert norm_w.shape[0] == 1, "single-layer trunk"
    p = _prepare(norm_w[0], w_in[0], b_gate[0], conv_w[0], conv_b[0], mh_norm_w[0], rpb[0],
                 w_down_a[0], w_down_b[0], w_out[0], final_norm_w)
    return (_trunk(x_prompt, p), _trunk(x_sample, p))
```

```python
import functools

import numpy as np
import jax
import jax.numpy as jnp
from jax import lax
from jax.experimental import pallas as pl
from jax.experimental.pallas import tpu as pltpu

D_MODEL = 1024
GRID_W = 64
M_HEADS = 4
M_HEAD_DIM = 256
M_WIDTH = M_HEADS * M_HEAD_DIM
CONV_W = 5
N_GATES = 4 * M_HEADS
NA_HEADS = 16
NA_HEAD_DIM = 64
NA_WIDTH = NA_HEADS * NA_HEAD_DIM
NA_MAX_KH = 8
NA_KW = 16
NA_QCB = 16
NA_KCB = 32
EPS = 1e-6
NEG = -1e30
LOG2E = 1.4426950408889634

LANES = 128
SUBLANES = 8
BF16_SUBLANES = 16
VMEM_LIMIT_BYTES = 56 * 1024 * 1024

STEP_MV, STEP_MQ, STEP_MK = 0, 1, 2
N_STEPS = 11
COL_MQ, COL_MK, COL_MO, COL_MZ, COL_NQ, COL_NK, COL_NV, COL_NZ, COL_GA, COL_GB = range(10)
D_MAIN = 10 * D_MODEL

PROJ_TM = 2048
HALO = BF16_SUBLANES
CONV_PIECE = 256
NORM_PIECE = 512
MLSTM_L = 256
GATES_TT = 2048
M_AUG = M_HEAD_DIM + BF16_SUBLANES
NA_RB = 4
NA_WIN = NA_RB + NA_MAX_KH
NA_BPS = 4
NA_HPB = 4
NA_LANES = NA_HPB * NA_HEAD_DIM
OUT_TM = 512
OUT_SUB = 256


def _dot(a, b):
    return jnp.dot(a, b, preferred_element_type=jnp.float32)


def _dot_nt(a, b):
    return lax.dot_general(a, b, (((1,), (1,)), ((), ())), preferred_element_type=jnp.float32)


def _sigmoid(x):
    return 1.0 / (1.0 + jnp.exp(-x))


def _silu(x):
    return x / (1.0 + jnp.exp(-x))


def _log_sigmoid(x):
    return jnp.minimum(x, 0.0) - jnp.log1p(jnp.exp(-jnp.abs(x)))


def _proj_kernel(x_ref, xp_ref, xn_ref_, nw_ref, w_ref, wgt_ref, bgt_ref, cw_ref, cb_ref,
                 o_ref, vt_ref, gt_ref, xn_s, acc_s):
    i = pl.program_id(1)
    j = pl.program_id(2)
    ni = pl.num_programs(1)
    tm = x_ref.shape[1]
    pad = CONV_W // 2

    def norm(x):
        y = x * lax.rsqrt(jnp.mean(x * x, axis=-1, keepdims=True) + EPS)
        return y * nw_ref[...]

    @pl.when(j == STEP_MV)
    def _():
        prev = jnp.where(i > 0, norm(xp_ref[0]), 0.0).astype(jnp.bfloat16)
        nxt = jnp.where(i < ni - 1, norm(xn_ref_[0]), 0.0).astype(jnp.bfloat16)
        xn_s[0:HALO, :] = prev
        xn_s[HALO + tm:2 * HALO + tm, :] = nxt
        for r0 in range(0, tm, NORM_PIECE):
            cur = norm(x_ref[0, r0:r0 + NORM_PIECE, :]).astype(jnp.bfloat16)
            xn_s[HALO + r0:HALO + r0 + NORM_PIECE, :] = cur
            gt_ref[0, :, r0:r0 + NORM_PIECE] = _dot_nt(wgt_ref[...], cur) + bgt_ref[...]
            vt_ref[0, :, r0:r0 + NORM_PIECE] = _dot_nt(w_ref[...], cur).astype(vt_ref.dtype)

    @pl.when((j == STEP_MQ) | (j == STEP_MK))
    def _():
        scale = jnp.where(j == STEP_MK, M_HEAD_DIM ** -0.5, 1.0).astype(o_ref.dtype)
        for c in range(tm // CONV_PIECE):
            r0 = c * CONV_PIECE
            acc_s[c % 2] = _dot(xn_s[r0:r0 + CONV_PIECE + 2 * HALO, :], w_ref[...])
            y = cb_ref[...]
            for t in range(CONV_W):
                y = y + acc_s[c % 2, pl.ds(HALO - pad + t, CONV_PIECE), :] * cw_ref[t:t + 1, :]
            o_ref[0, r0:r0 + CONV_PIECE, :] = _silu(y).astype(o_ref.dtype) * scale

    @pl.when(j > STEP_MK)
    def _():
        o_ref[0] = _dot(xn_s[HALO:HALO + tm, :], w_ref[...]).astype(o_ref.dtype)


def _proj(x, norm_w, w_steps, w_gt, b_gt, conv_w, conv_b):
    B, T, D = x.shape
    tm = min(PROJ_TM, T)
    assert T % tm == 0 and tm % CONV_PIECE == 0 and tm % NORM_PIECE == 0
    ni = T // tm
    hb = tm // HALO
    return pl.pallas_call(
        _proj_kernel,
        grid=(B, ni, N_STEPS),
        in_specs=[
            pl.BlockSpec((1, tm, D), lambda b, i, j: (b, i, 0)),
            pl.BlockSpec((1, HALO, D), lambda b, i, j: (b, jnp.maximum(i * hb - 1, 0), 0)),
            pl.BlockSpec((1, HALO, D), lambda b, i, j: (b, jnp.minimum((i + 1) * hb, T // HALO - 1), 0)),
            pl.BlockSpec((1, D), lambda b, i, j: (0, 0)),
            pl.BlockSpec((D, D_MODEL), lambda b, i, j: (0, j)),
            pl.BlockSpec((N_GATES, D), lambda b, i, j: (0, 0)),
            pl.BlockSpec((N_GATES, 1), lambda b, i, j: (0, 0)),
            pl.BlockSpec((SUBLANES, D_MODEL), lambda b, i, j: (0, jnp.clip(j - STEP_MQ, 0, 1))),
            pl.BlockSpec((1, D_MODEL), lambda b, i, j: (0, jnp.clip(j - STEP_MQ, 0, 1))),
        ],
        out_specs=[
            pl.BlockSpec((1, tm, D_MODEL), lambda b, i, j: (b, i, jnp.maximum(j - STEP_MQ, 0))),
            pl.BlockSpec((1, M_WIDTH, tm), lambda b, i, j: (b, 0, i)),
            pl.BlockSpec((1, N_GATES, tm), lambda b, i, j: (b, 0, i)),
        ],
        out_shape=[
            jax.ShapeDtypeStruct((B, T, D_MAIN), jnp.bfloat16),
            jax.ShapeDtypeStruct((B, M_WIDTH, T), jnp.bfloat16),
            jax.ShapeDtypeStruct((B, N_GATES, T), jnp.float32),
        ],
        scratch_shapes=[
            pltpu.VMEM((tm + 2 * HALO, D), jnp.bfloat16),
            pltpu.VMEM((2, CONV_PIECE + 2 * HALO, D_MODEL), jnp.float32),
        ],
        compiler_params=pltpu.CompilerParams(
            dimension_semantics=("arbitrary", "arbitrary", "arbitrary"),
            vmem_limit_bytes=VMEM_LIMIT_BYTES),
        name="proj",
    )(x, x, x, norm_w, w_steps, w_gt, b_gt, conv_w, conv_b)


def _scan_lanes(x, pos, L, d, op, fill):
    n = x.shape[1]
    sh = 1
    while sh < L:
        if d == 0:
            x = op(x, jnp.where(pos >= sh, pltpu.roll(x, sh, axis=1), fill))
        else:
            x = op(x, jnp.where(pos < L - sh, pltpu.roll(x, n - sh, axis=1), fill))
        sh *= 2
    return x


def _gates_kernel(L, gt_ref, abc_ref, at_ref):
    n = gt_ref.shape[2]
    pos = lax.broadcasted_iota(jnp.int32, (SUBLANES, n), 1) % L
    a_rows = []
    for d in range(2):
        rows = slice(d * SUBLANES, (d + 1) * SUBLANES)
        g8 = gt_ref[0, rows, :]
        b8 = _scan_lanes(_log_sigmoid(g8), pos, L, d, jnp.add, 0.0)
        b8 = pltpu.roll(b8, M_HEADS, axis=0)
        a8 = g8 - b8
        abc_ref[0, d * SUBLANES:(d + 1) * SUBLANES, :] = a8
        abc_ref[0, N_GATES + d * SUBLANES:N_GATES + (d + 1) * SUBLANES, :] = b8
        abc_ref[0, 2 * N_GATES + d * SUBLANES:2 * N_GATES + (d + 1) * SUBLANES, :] = _scan_lanes(
            a8, pos, L, d, jnp.maximum, -jnp.inf)
        a_rows.append(a8)
    pad = jnp.zeros((LANES - 2 * SUBLANES, n), jnp.float32)
    at_ref[0] = jnp.concatenate(a_rows + [pad], axis=0).T


def _gates(g_t, L):
    B, _, T = g_t.shape
    tg = min(GATES_TT, T)
    assert T % tg == 0 and tg % L == 0
    return pl.pallas_call(
        functools.partial(_gates_kernel, L),
        grid=(B, T // tg),
        in_specs=[pl.BlockSpec((1, N_GATES, tg), lambda b, i: (b, 0, i))],
        out_specs=[pl.BlockSpec((1, 3 * N_GATES, tg), lambda b, i: (b, 0, i)),
                   pl.BlockSpec((1, tg, LANES), lambda b, i: (b, i, 0))],
        out_shape=[jax.ShapeDtypeStruct((B, 3 * N_GATES, T), jnp.float32),
                   jax.ShapeDtypeStruct((B, T, LANES), jnp.float32)],
        compiler_params=pltpu.CompilerParams(
            dimension_semantics=("arbitrary", "arbitrary"),
            vmem_limit_bytes=VMEM_LIMIT_BYTES),
        name="gates",
    )(g_t)


def _mlstm_kernel(qf_ref, kf_ref, vtf_ref, abcf_ref, atf_ref, qb_ref, kb_ref, vtb_ref, abcb_ref, atb_ref,
                  hf_ref, hb_ref, st_ref, m_ref):
    L = qf_ref.shape[1]
    bf16 = jnp.bfloat16

    @pl.when(pl.program_id(1) == 0)
    def _():
        st_ref[...] = jnp.zeros_like(st_ref)
        m_ref[...] = jnp.zeros_like(m_ref)

    row = lax.broadcasted_iota(jnp.int32, (L, L), 0)
    col = lax.broadcasted_iota(jnp.int32, (L, L), 1)
    ones_rows = jnp.ones((BF16_SUBLANES, L), bf16)
    dirs = ((qf_ref, kf_ref, vtf_ref, abcf_ref, atf_ref, hf_ref),
            (qb_ref, kb_ref, vtb_ref, abcb_ref, atb_ref, hb_ref))
    pre = []
    for d, (_, _, _, abc_ref, _, _) in enumerate(dirs):
        last = L - 1 if d == 0 else 0
        a8, b8, cm8 = (abc_ref[0, kind * N_GATES + d * SUBLANES:kind * N_GATES + (d + 1) * SUBLANES, :]
                       for kind in range(3))
        m8 = m_ref[d]
        c8 = jnp.maximum(cm8, m8)
        c_last = c8[:, last:last + 1]
        m_ref[d] = b8[:, last:last + 1] + c_last
        pre.append(dict(
            c=c8, w=jnp.exp(m8 - c8), floor=jnp.exp(-b8 - c8), wk=jnp.exp(a8 - c_last).astype(bf16),
            decay=jnp.exp(m8 - c_last),
            mask=(row <= col) if d == 0 else (row >= col)))

    for h in range(M_HEADS):
        for d, (q_ref, k_ref, vt_ref, _, at_ref, ht_ref) in enumerate(dirs):
            p = pre[d]
            idx = d * M_HEADS + h
            lanes = slice(h * M_HEAD_DIM, (h + 1) * M_HEAD_DIM)
            q = q_ref[0, :, lanes]
            k = k_ref[0, :, lanes]
            vt = jnp.concatenate([vt_ref[0, lanes, :], ones_rows], axis=0)
            st = st_ref[idx]
            a_col = at_ref[0, :, d * SUBLANES + h:d * SUBLANES + h + 1]

            e = jnp.exp(jnp.where(p["mask"], a_col - p["c"][h:h + 1, :], -jnp.inf))
            pt = (_dot_nt(k, q) * e).astype(bf16)
            num = _dot(vt, pt) + p["w"][h:h + 1, :] * _dot_nt(st.astype(bf16), q)
            den = num[M_HEAD_DIM:M_HEAD_DIM + 1, :]
            ht_ref[0, lanes, :] = (num[:M_HEAD_DIM]
                                   / jnp.maximum(jnp.abs(den), p["floor"][h:h + 1, :])).astype(ht_ref.dtype)

    for h in range(M_HEADS):
        for d, (_, k_ref, vt_ref, _, _, _) in enumerate(dirs):
            p = pre[d]
            idx = d * M_HEADS + h
            lanes = slice(h * M_HEAD_DIM, (h + 1) * M_HEAD_DIM)
            vt = jnp.concatenate([vt_ref[0, lanes, :], ones_rows], axis=0)
            st_ref[idx] = (p["decay"][h:h + 1, :] * st_ref[idx]
                           + _dot(vt * p["wk"][h:h + 1, :], k_ref[0, :, lanes]))


def _mlstm(proj, v_t, gates):
    B, T, _ = proj.shape
    L = min(MLSTM_L, T)
    assert T % L == 0
    nc = T // L
    abc, a_t = gates

    def tok(width, col, rev):
        return pl.BlockSpec((1, L, width), lambda b, c: (b, nc - 1 - c if rev else c, col))

    def feat(rows, rev):
        return pl.BlockSpec((1, rows, L), lambda b, c: (b, 0, nc - 1 - c if rev else c))

    def side(rev):
        return [tok(M_WIDTH, COL_MQ, rev), tok(M_WIDTH, COL_MK, rev), feat(M_WIDTH, rev),
                feat(3 * N_GATES, rev), tok(LANES, 0, rev)]

    return pl.pallas_call(
        _mlstm_kernel,
        grid=(B, nc),
        in_specs=side(False) + side(True),
        out_specs=[feat(M_WIDTH, False), feat(M_WIDTH, True)],
        out_shape=[
            jax.ShapeDtypeStruct((B, M_WIDTH, T), jnp.bfloat16),
            jax.ShapeDtypeStruct((B, M_WIDTH, T), jnp.bfloat16),
        ],
        scratch_shapes=[
            pltpu.VMEM((2 * M_HEADS, M_AUG, M_HEAD_DIM), jnp.float32),
            pltpu.VMEM((2, SUBLANES, 1), jnp.float32),
        ],
        compiler_params=pltpu.CompilerParams(
            dimension_semantics=("arbitrary", "arbitrary"),
            vmem_limit_bytes=VMEM_LIMIT_BYTES),
        name="mlstm",
    )(proj, proj, v_t, abc, a_t, proj, proj, v_t, abc, a_t)


_NA_KB = tuple(int(v) for v in np.clip(np.arange(GRID_W // NA_QCB) * NA_QCB - NA_KW // 2, 0, GRID_W - NA_KCB))
NA_NCB = GRID_W // NA_QCB
NA_MQ = NA_RB * NA_QCB
NA_NK = NA_WIN * NA_KCB


def _na_bias_table(rpb):
    n_dr, n_dc = 2 * NA_MAX_KH - 1, 2 * NA_KW - 1
    i = np.arange(NA_RB)[:, None]
    jr = np.arange(NA_WIN)[None, :]
    row_sel = np.zeros((3, NA_RB, NA_WIN, n_dr), np.float32)
    for rt in range(3):
        off = (0, -NA_MAX_KH // 2, NA_RB - NA_WIN)[rt]
        if rt == 0:
            rs = np.maximum(i - NA_MAX_KH // 2, 0)
        elif rt == 1:
            rs = i - NA_MAX_KH // 2
        else:
            rs = np.minimum(i - NA_MAX_KH // 2, NA_RB - NA_MAX_KH)
        krow = off + jr
        ii, jj = np.nonzero((krow >= rs) & (krow < rs + NA_MAX_KH))
        row_sel[rt, ii, jj, (krow - i)[ii, jj] + NA_MAX_KH - 1] = 1.0
    qc = np.arange(NA_QCB)[:, None]
    kc = np.arange(NA_KCB)[None, :]
    col_sel = np.zeros((NA_NCB, NA_QCB, NA_KCB, n_dc), np.float32)
    for n in range(NA_NCB):
        qcol = n * NA_QCB + qc
        kcol = _NA_KB[n] + kc
        cs = np.clip(qcol - NA_KW // 2, 0, GRID_W - NA_KW)
        qq, kk = np.nonzero((kcol >= cs) & (kcol < cs + NA_KW))
        col_sel[n, qq, kk, (kcol - qcol)[qq, kk] + NA_KW - 1] = 1.0
    ok = (np.einsum("xijr->xij", row_sel)[:, None, :, None, :, None]
          * np.einsum("nqkc->nqk", col_sel)[None, :, None, :, None, :]) > 0
    hi = lax.Precision.HIGHEST
    r = rpb.astype(jnp.float32).reshape(NA_HEADS // NA_HPB, NA_HPB, n_dr, n_dc)
    a = jnp.einsum("ghrc,xijr->gxhijc", r, row_sel, precision=hi)
    t = jnp.einsum("gxhijc,nqkc->gxnhiqjk", a, col_sel, precision=hi)
    t = jnp.where(ok[None, :, :, None], t * LOG2E, NEG)
    return t.reshape(NA_HEADS // NA_HPB, 3 * NA_NCB, NA_HPB * NA_MQ, NA_NK)


def _natten_kernel(q_ref, *refs):
    kv_refs, (z_ref, tab_ref, o_ref, acc_s) = refs[:-4], refs[-4:]
    nrb = pl.num_programs(2) * NA_BPS
    tq = NA_RB * GRID_W
    lane = lax.broadcasted_iota(jnp.int32, (NA_MQ, NA_LANES), 1) // NA_HEAD_DIM

    def keys(xf, n):
        kb = _NA_KB[n]
        return jnp.concatenate([xf[j * GRID_W + kb:j * GRID_W + kb + NA_KCB] for j in range(NA_WIN)],
                               axis=0).astype(jnp.bfloat16)

    for blk in range(NA_BPS):
        rb = pl.program_id(2) * NA_BPS + blk
        rt = jnp.where(rb == 0, 0, jnp.where(rb == nrb - 1, 2, 1))
        kf = kv_refs[2 * blk][0].astype(jnp.float32)
        vf = kv_refs[2 * blk + 1][0].astype(jnp.float32)
        q = q_ref[0, blk * tq:(blk + 1) * tq, :]
        for n in range(NA_NCB):
            qn = jnp.concatenate([q[i * GRID_W + n * NA_QCB:i * GRID_W + (n + 1) * NA_QCB] for i in range(NA_RB)],
                                 axis=0)
            qs = jnp.concatenate([jnp.where(lane == hh, qn, 0.0).astype(jnp.bfloat16) for hh in range(NA_HPB)],
                                 axis=0)
            s = _dot_nt(qs, keys(kf, n)) + tab_ref[0, rt * NA_NCB + n]
            m = jnp.max(s, axis=1, keepdims=True)
            e = jnp.exp2(s - m)
            l = jnp.sum(e, axis=1, keepdims=True)
            o = _dot(e.astype(jnp.bfloat16), keys(vf, n)) / l
            on = jnp.zeros((NA_MQ, NA_LANES), jnp.float32)
            for hh in range(NA_HPB):
                on = jnp.where(lane == hh, o[hh * NA_MQ:(hh + 1) * NA_MQ], on)
            for i in range(NA_RB):
                r0 = blk * tq + i * GRID_W + n * NA_QCB
                acc_s[r0:r0 + NA_QCB, :] = on[i * NA_QCB:(i + 1) * NA_QCB]
    o_ref[0] = (acc_s[...] * _silu(z_ref[0].astype(jnp.float32))).astype(o_ref.dtype)


def _natten(proj, table):
    B, T, _ = proj.shape
    rows = T // GRID_W
    assert T % GRID_W == 0 and rows % (NA_RB * NA_BPS) == 0 and rows >= NA_WIN
    ng = NA_HEADS // NA_HPB
    tq = NA_BPS * NA_RB * GRID_W
    cpt = D_MODEL // NA_LANES

    def win(col, blk):
        def index(g, b, s):
            start = jnp.clip((s * NA_BPS + blk) * NA_RB - NA_MAX_KH // 2, 0, rows - NA_WIN)
            return (b, start * GRID_W, (col * cpt + g) * NA_LANES)
        return pl.BlockSpec((pl.Element(1), pl.Element(NA_WIN * GRID_W), pl.Element(NA_LANES)), index)

    windows = []
    for blk in range(NA_BPS):
        windows += [win(COL_NK, blk), win(COL_NV, blk)]
    return pl.pallas_call(
        _natten_kernel,
        grid=(ng, B, rows // (NA_RB * NA_BPS)),
        in_specs=[pl.BlockSpec((1, tq, NA_LANES), lambda g, b, s: (b, s, COL_NQ * cpt + g))]
        + windows
        + [pl.BlockSpec((1, tq, NA_LANES), lambda g, b, s: (b, s, COL_NZ * cpt + g)),
           pl.BlockSpec((1, 3 * NA_NCB, NA_HPB * NA_MQ, NA_NK), lambda g, b, s: (g, 0, 0, 0))],
        out_specs=pl.BlockSpec((1, tq, NA_LANES), lambda g, b, s: (b, s, g)),
        out_shape=jax.ShapeDtypeStruct((B, T, NA_WIDTH), jnp.bfloat16),
        scratch_shapes=[pltpu.VMEM((tq, NA_LANES), jnp.float32)],
        compiler_params=pltpu.CompilerParams(
            dimension_semantics=("arbitrary", "arbitrary", "arbitrary"),
            vmem_limit_bytes=VMEM_LIMIT_BYTES),
        name="natten",
    )(proj, *([proj] * (2 * NA_BPS)), proj, table)


def _out_kernel(x_ref, hf_ref, hb_ref, o_ref, z_ref, yb_ref, ga_ref, gb_ref, mhw_ref, wda_ref, wdb_ref, wo_ref,
                fnw_ref, y_ref):
    f32 = jnp.float32
    tm = x_ref.shape[1]
    for r0 in range(0, tm, OUT_SUB):
        rows = slice(r0, r0 + OUT_SUB)
        h = hf_ref[0, :, rows].astype(f32) + hb_ref[0, :, rows].astype(f32)
        h = h.T * _sigmoid(o_ref[0, rows, :].astype(f32))
        parts = []
        for hd in range(M_HEADS):
            hh = h[:, hd * M_HEAD_DIM:(hd + 1) * M_HEAD_DIM]
            parts.append(hh * lax.rsqrt(jnp.mean(hh * hh, axis=-1, keepdims=True) + EPS))
        hn = jnp.concatenate(parts, axis=-1) * mhw_ref[...]
        ya = (hn * _silu(z_ref[0, rows, :].astype(f32))).astype(jnp.bfloat16)
        merged = (_sigmoid(ga_ref[0, rows, :].astype(f32)) * _dot(ya, wda_ref[...])
                  + _sigmoid(gb_ref[0, rows, :].astype(f32)) * _dot(yb_ref[0, rows, :], wdb_ref[...]))
        y = x_ref[0, rows, :] + _dot(merged.astype(jnp.bfloat16), wo_ref[...])
        y = y * lax.rsqrt(jnp.mean(y * y, axis=-1, keepdims=True) + EPS)
        y_ref[0, rows, :] = y * fnw_ref[...]


def _out(x, h_f, h_b, proj, y_b, mh_w, w_da, w_db, w_o, fn_w):
    B, T, D = x.shape
    tm = min(OUT_TM, T)
    assert T % tm == 0

    def tok(col=0):
        return pl.BlockSpec((1, tm, D_MODEL), lambda b, i, col=col: (b, i, col))

    def feat():
        return pl.BlockSpec((1, M_WIDTH, tm), lambda b, i: (b, 0, i))

    def const(shape):
        return pl.BlockSpec(shape, lambda b, i: (0,) * len(shape))

    return pl.pallas_call(
        _out_kernel,
        grid=(B, T // tm),
        in_specs=[tok(), feat(), feat(), tok(COL_MO), tok(COL_MZ), tok(), tok(COL_GA), tok(COL_GB),
                  const((1, D)), const((D, D)), const((D, D)), const((D, D)), const((1, D))],
        out_specs=tok(),
        out_shape=jax.ShapeDtypeStruct((B, T, D), jnp.float32),
        compiler_params=pltpu.CompilerParams(
            dimension_semantics=("arbitrary", "arbitrary"),
            vmem_limit_bytes=VMEM_LIMIT_BYTES),
        name="out",
    )(x, h_f, h_b, proj, proj, y_b, proj, proj, mh_w, w_da, w_db, w_o, fn_w)


def _prepare(norm_w, w_in, b_gate, conv_w, conv_b, mh_norm_w, rpb, w_down_a, w_down_b, w_out, final_norm_w):
    bf16 = jnp.bfloat16
    g0 = 5 * M_WIDTH
    v0 = 2 * M_WIDTH
    nq0 = g0 + N_GATES
    w_steps = jnp.concatenate([w_in[:, v0:v0 + M_WIDTH].T, w_in[:, :v0], w_in[:, v0 + M_WIDTH:g0],
                               w_in[:, nq0:nq0 + NA_WIDTH] * (LOG2E * NA_HEAD_DIM ** -0.5),
                               w_in[:, nq0 + NA_WIDTH:]], axis=1).astype(bf16)
    w_gt = w_in[:, g0:g0 + N_GATES].T.astype(bf16)
    cw = jnp.pad(conv_w, ((0, SUBLANES - CONV_W), (0, 0)))
    return dict(
        norm_w=norm_w.reshape(1, D_MODEL), w_steps=w_steps, w_gt=w_gt, b_gt=b_gate.reshape(N_GATES, 1),
        conv_w=cw, conv_b=conv_b.reshape(1, 2 * M_WIDTH), mh_w=mh_norm_w.reshape(1, M_WIDTH),
        table=_na_bias_table(rpb), w_da=w_down_a.astype(bf16), w_db=w_down_b.astype(bf16),
        w_o=w_out.astype(bf16), fn_w=final_norm_w.reshape(1, D_MODEL))


def _trunk(x, p):
    proj, v_t, g_t = _proj(x, p["norm_w"], p["w_steps"], p["w_gt"], p["b_gt"], p["conv_w"], p["conv_b"])
    h_f, h_b = _mlstm(proj, v_t, _gates(g_t, min(MLSTM_L, x.shape[1])))
    y_b = _natten(proj, p["table"])
    return _out(x, h_f, h_b, proj, y_b, p["mh_w"], p["w_da"], p["w_db"], p["w_o"], p["fn_w"])


def kernel(x_prompt, x_sample, norm_w, w_in, b_gate, conv_w, conv_b, mh_norm_w, rpb, w_down_a, w_down_b, w_out,
           final_norm_w):
    assert norm_w.shape[0] == 1, "single-layer trunk"
    p = _prepare(norm_w[0], w_in[0], b_gate[0], conv_w[0], conv_b[0], mh_norm_w[0], rpb[0],
                 w_down_a[0], w_down_b[0], w_out[0], final_norm_w)
    return (_trunk(x_prompt, p), _trunk(x_sample, p))
```

```python
import functools

import numpy as np
import jax
import jax.numpy as jnp
from jax import lax
from jax.experimental import pallas as pl
from jax.experimental.pallas import tpu as pltpu

D_MODEL = 1024
GRID_W = 64
M_HEADS = 4
M_HEAD_DIM = 256
M_WIDTH = M_HEADS * M_HEAD_DIM
CONV_W = 5
N_GATES = 4 * M_HEADS
NA_HEADS = 16
NA_HEAD_DIM = 64
NA_WIDTH = NA_HEADS * NA_HEAD_DIM
NA_MAX_KH = 8
NA_KW = 16
NA_QCB = 16
NA_KCB = 32
EPS = 1e-6
NEG = -1e30
LOG2E = 1.4426950408889634

LANES = 128
SUBLANES = 8
BF16_SUBLANES = 16
VMEM_LIMIT_BYTES = 56 * 1024 * 1024

STEP_MV, STEP_MQ, STEP_MK = 0, 1, 2
N_STEPS = 11
COL_MQ, COL_MK, COL_MO, COL_MZ, COL_NQ, COL_NK, COL_NV, COL_NZ, COL_GA, COL_GB = range(10)
D_MAIN = 10 * D_MODEL

PROJ_TM = 2048
HALO = BF16_SUBLANES
CONV_PIECE = 512
NORM_PIECE = 512
MLSTM_L = 256
GATES_TT = 2048
M_AUG = M_HEAD_DIM + BF16_SUBLANES
NA_RB = 4
NA_WIN = NA_RB + NA_MAX_KH
NA_BPS = 8
NA_HPB = 4
NA_LANES = NA_HPB * NA_HEAD_DIM
OUT_TM = 512
OUT_SUB = 256


def _dot(a, b):
    return jnp.dot(a, b, preferred_element_type=jnp.float32)


def _dot_nt(a, b):
    return lax.dot_general(a, b, (((1,), (1,)), ((), ())), preferred_element_type=jnp.float32)


def _sigmoid(x):
    return 1.0 / (1.0 + jnp.exp2(x * -LOG2E))


def _silu(x):
    return x / (1.0 + jnp.exp2(x * -LOG2E))


def _log_sigmoid(x):
    return jnp.minimum(x, 0.0) - jnp.log1p(jnp.exp(-jnp.abs(x)))


def _proj_kernel(x_ref, xp_ref, xn_ref_, nw_ref, w_ref, wgt_ref, bgt_ref, cw_ref, cb_ref,
                 o_ref, vt_ref, gt_ref, xn_s, acc_s):
    i = pl.program_id(1)
    j = pl.program_id(2)
    ni = pl.num_programs(1)
    tm = x_ref.shape[1]
    pad = CONV_W // 2

    def norm(x):
        y = x * lax.rsqrt(jnp.mean(x * x, axis=-1, keepdims=True) + EPS)
        return y * nw_ref[...]

    @pl.when(j == STEP_MV)
    def _():
        prev = jnp.where(i > 0, norm(xp_ref[0]), 0.0).astype(jnp.bfloat16)
        nxt = jnp.where(i < ni - 1, norm(xn_ref_[0]), 0.0).astype(jnp.bfloat16)
        xn_s[0:HALO, :] = prev
        xn_s[HALO + tm:2 * HALO + tm, :] = nxt
        for r0 in range(0, tm, NORM_PIECE):
            cur = norm(x_ref[0, r0:r0 + NORM_PIECE, :]).astype(jnp.bfloat16)
            xn_s[HALO + r0:HALO + r0 + NORM_PIECE, :] = cur
            gt_ref[0, :, r0:r0 + NORM_PIECE] = _dot_nt(wgt_ref[...], cur) + bgt_ref[...]
            vt_ref[0, :, r0:r0 + NORM_PIECE] = _dot_nt(w_ref[...], cur).astype(vt_ref.dtype)

    @pl.when((j == STEP_MQ) | (j == STEP_MK))
    def _():
        scale = jnp.where(j == STEP_MK, M_HEAD_DIM ** -0.5, 1.0).astype(o_ref.dtype)
        for c in range(tm // CONV_PIECE):
            r0 = c * CONV_PIECE
            acc_s[c % 2] = _dot(xn_s[r0:r0 + CONV_PIECE + 2 * HALO, :], w_ref[...])
            y = cb_ref[...]
            for t in range(CONV_W):
                y = y + acc_s[c % 2, pl.ds(HALO - pad + t, CONV_PIECE), :] * cw_ref[t:t + 1, :]
            o_ref[0, r0:r0 + CONV_PIECE, :] = _silu(y).astype(o_ref.dtype) * scale

    @pl.when(j > STEP_MK)
    def _():
        o_ref[0] = _dot(xn_s[HALO:HALO + tm, :], w_ref[...]).astype(o_ref.dtype)


def _proj(x, norm_w, w_steps, w_gt, b_gt, conv_w, conv_b):
    B, T, D = x.shape
    tm = min(PROJ_TM, T)
    assert T % tm == 0 and tm % CONV_PIECE == 0 and tm % NORM_PIECE == 0
    ni = T // tm
    hb = tm // HALO
    return pl.pallas_call(
        _proj_kernel,
        grid=(B, ni, N_STEPS),
        in_specs=[
            pl.BlockSpec((1, tm, D), lambda b, i, j: (b, i, 0)),
            pl.BlockSpec((1, HALO, D), lambda b, i, j: (b, jnp.maximum(i * hb - 1, 0), 0)),
            pl.BlockSpec((1, HALO, D), lambda b, i, j: (b, jnp.minimum((i + 1) * hb, T // HALO - 1), 0)),
            pl.BlockSpec((1, D), lambda b, i, j: (0, 0)),
            pl.BlockSpec((D, D_MODEL), lambda b, i, j: (0, j)),
            pl.BlockSpec((N_GATES, D), lambda b, i, j: (0, 0)),
            pl.BlockSpec((N_GATES, 1), lambda b, i, j: (0, 0)),
            pl.BlockSpec((SUBLANES, D_MODEL), lambda b, i, j: (0, jnp.clip(j - STEP_MQ, 0, 1))),
            pl.BlockSpec((1, D_MODEL), lambda b, i, j: (0, jnp.clip(j - STEP_MQ, 0, 1))),
        ],
        out_specs=[
            pl.BlockSpec((1, tm, D_MODEL), lambda b, i, j: (b, i, jnp.maximum(j - STEP_MQ, 0))),
            pl.BlockSpec((1, M_WIDTH, tm), lambda b, i, j: (b, 0, i)),
            pl.BlockSpec((1, N_GATES, tm), lambda b, i, j: (b, 0, i)),
        ],
        out_shape=[
            jax.ShapeDtypeStruct((B, T, D_MAIN), jnp.bfloat16),
            jax.ShapeDtypeStruct((B, M_WIDTH, T), jnp.bfloat16),
            jax.ShapeDtypeStruct((B, N_GATES, T), jnp.float32),
        ],
        scratch_shapes=[
            pltpu.VMEM((tm + 2 * HALO, D), jnp.bfloat16),
            pltpu.VMEM((2, CONV_PIECE + 2 * HALO, D_MODEL), jnp.float32),
        ],
        compiler_params=pltpu.CompilerParams(
            dimension_semantics=("arbitrary", "arbitrary", "arbitrary"),
            vmem_limit_bytes=VMEM_LIMIT_BYTES),
        name="proj",
    )(x, x, x, norm_w, w_steps, w_gt, b_gt, conv_w, conv_b)


def _scan_lanes(x, pos, L, d, op, fill):
    n = x.shape[1]
    sh = 1
    while sh < L:
        if d == 0:
            x = op(x, jnp.where(pos >= sh, pltpu.roll(x, sh, axis=1), fill))
        else:
            x = op(x, jnp.where(pos < L - sh, pltpu.roll(x, n - sh, axis=1), fill))
        sh *= 2
    return x


def _gates_kernel(L, gt_ref, abc_ref, at_ref):
    n = gt_ref.shape[2]
    pos = lax.broadcasted_iota(jnp.int32, (SUBLANES, n), 1) % L
    a_rows = []
    for d in range(2):
        rows = slice(d * SUBLANES, (d + 1) * SUBLANES)
        g8 = gt_ref[0, rows, :]
        b8 = _scan_lanes(_log_sigmoid(g8), pos, L, d, jnp.add, 0.0)
        b8 = pltpu.roll(b8, M_HEADS, axis=0)
        a8 = g8 - b8
        abc_ref[0, d * SUBLANES:(d + 1) * SUBLANES, :] = a8
        abc_ref[0, N_GATES + d * SUBLANES:N_GATES + (d + 1) * SUBLANES, :] = b8
        abc_ref[0, 2 * N_GATES + d * SUBLANES:2 * N_GATES + (d + 1) * SUBLANES, :] = _scan_lanes(
            a8, pos, L, d, jnp.maximum, -jnp.inf)
        a_rows.append(a8)
    pad = jnp.zeros((LANES - 2 * SUBLANES, n), jnp.float32)
    at_ref[0] = jnp.concatenate(a_rows + [pad], axis=0).T


def _gates(g_t, L):
    B, _, T = g_t.shape
    tg = min(GATES_TT, T)
    assert T % tg == 0 and tg % L == 0
    return pl.pallas_call(
        functools.partial(_gates_kernel, L),
        grid=(B, T // tg),
        in_specs=[pl.BlockSpec((1, N_GATES, tg), lambda b, i: (b, 0, i))],
        out_specs=[pl.BlockSpec((1, 3 * N_GATES, tg), lambda b, i: (b, 0, i)),
                   pl.BlockSpec((1, tg, LANES), lambda b, i: (b, i, 0))],
        out_shape=[jax.ShapeDtypeStruct((B, 3 * N_GATES, T), jnp.float32),
                   jax.ShapeDtypeStruct((B, T, LANES), jnp.float32)],
        compiler_params=pltpu.CompilerParams(
            dimension_semantics=("arbitrary", "arbitrary"),
            vmem_limit_bytes=VMEM_LIMIT_BYTES),
        name="gates",
    )(g_t)


def _mlstm_kernel(qf_ref, kf_ref, vtf_ref, abcf_ref, atf_ref, qb_ref, kb_ref, vtb_ref, abcb_ref, atb_ref,
                  hf_ref, hb_ref, st_ref, m_ref):
    L = qf_ref.shape[1]
    bf16 = jnp.bfloat16

    @pl.when(pl.program_id(1) == 0)
    def _():
        st_ref[...] = jnp.zeros_like(st_ref)
        m_ref[...] = jnp.zeros_like(m_ref)

    row = lax.broadcasted_iota(jnp.int32, (L, L), 0)
    col = lax.broadcasted_iota(jnp.int32, (L, L), 1)
    ones_rows = jnp.ones((BF16_SUBLANES, L), bf16)
    dirs = ((qf_ref, kf_ref, vtf_ref, abcf_ref, atf_ref, hf_ref),
            (qb_ref, kb_ref, vtb_ref, abcb_ref, atb_ref, hb_ref))
    pre = []
    for d, (_, _, _, abc_ref, _, _) in enumerate(dirs):
        last = L - 1 if d == 0 else 0
        a8, b8, cm8 = (abc_ref[0, kind * N_GATES + d * SUBLANES:kind * N_GATES + (d + 1) * SUBLANES, :]
                       for kind in range(3))
        m8 = m_ref[d]
        c8 = jnp.maximum(cm8, m8)
        c_last = c8[:, last:last + 1]
        m_ref[d] = b8[:, last:last + 1] + c_last
        pre.append(dict(
            c=c8, w=jnp.exp(m8 - c8), floor=jnp.exp(-b8 - c8), wk=jnp.exp(a8 - c_last).astype(bf16),
            decay=jnp.exp(m8 - c_last),
            mask=(row <= col) if d == 0 else (row >= col)))

    for h in range(M_HEADS):
        for d, (q_ref, k_ref, vt_ref, _, at_ref, ht_ref) in enumerate(dirs):
            p = pre[d]
            idx = d * M_HEADS + h
            lanes = slice(h * M_HEAD_DIM, (h + 1) * M_HEAD_DIM)
            q = q_ref[0, :, lanes]
            k = k_ref[0, :, lanes]
            vt = jnp.concatenate([vt_ref[0, lanes, :], ones_rows], axis=0)
            st = st_ref[idx]
            a_col = at_ref[0, :, d * SUBLANES + h:d * SUBLANES + h + 1]

            e = jnp.exp(jnp.where(p["mask"], a_col - p["c"][h:h + 1, :], -jnp.inf))
            pt = (_dot_nt(k, q) * e).astype(bf16)
            num = _dot(vt, pt) + p["w"][h:h + 1, :] * _dot_nt(st.astype(bf16), q)
            den = num[M_HEAD_DIM:M_HEAD_DIM + 1, :]
            ht_ref[0, lanes, :] = (num[:M_HEAD_DIM]
                                   / jnp.maximum(jnp.abs(den), p["floor"][h:h + 1, :])).astype(ht_ref.dtype)

    for h in range(M_HEADS):
        for d, (_, k_ref, vt_ref, _, _, _) in enumerate(dirs):
            p = pre[d]
            idx = d * M_HEADS + h
            lanes = slice(h * M_HEAD_DIM, (h + 1) * M_HEAD_DIM)
            vt = jnp.concatenate([vt_ref[0, lanes, :], ones_rows], axis=0)
            st_ref[idx] = (p["decay"][h:h + 1, :] * st_ref[idx]
                           + _dot(vt * p["wk"][h:h + 1, :], k_ref[0, :, lanes]))


def _mlstm(proj, v_t, gates):
    B, T, _ = proj.shape
    L = min(MLSTM_L, T)
    assert T % L == 0
    nc = T // L
    abc, a_t = gates

    def tok(width, col, rev):
        return pl.BlockSpec((1, L, width), lambda b, c: (b, nc - 1 - c if rev else c, col))

    def feat(rows, rev):
        return pl.BlockSpec((1, rows, L), lambda b, c: (b, 0, nc - 1 - c if rev else c))

    def side(rev):
        return [tok(M_WIDTH, COL_MQ, rev), tok(M_WIDTH, COL_MK, rev), feat(M_WIDTH, rev),
                feat(3 * N_GATES, rev), tok(LANES, 0, rev)]

    return pl.pallas_call(
        _mlstm_kernel,
        grid=(B, nc),
        in_specs=side(False) + side(True),
        out_specs=[feat(M_WIDTH, False), feat(M_WIDTH, True)],
        out_shape=[
            jax.ShapeDtypeStruct((B, M_WIDTH, T), jnp.bfloat16),
            jax.ShapeDtypeStruct((B, M_WIDTH, T), jnp.bfloat16),
        ],
        scratch_shapes=[
            pltpu.VMEM((2 * M_HEADS, M_AUG, M_HEAD_DIM), jnp.float32),
            pltpu.VMEM((2, SUBLANES, 1), jnp.float32),
        ],
        compiler_params=pltpu.CompilerParams(
            dimension_semantics=("arbitrary", "arbitrary"),
            vmem_limit_bytes=VMEM_LIMIT_BYTES),
        name="mlstm",
    )(proj, proj, v_t, abc, a_t, proj, proj, v_t, abc, a_t)


_NA_KB = tuple(int(v) for v in np.clip(np.arange(GRID_W // NA_QCB) * NA_QCB - NA_KW // 2, 0, GRID_W - NA_KCB))
NA_NCB = GRID_W // NA_QCB
NA_MQ = NA_RB * NA_QCB
NA_NK = NA_WIN * NA_KCB


def _na_bias_table(rpb):
    n_dr, n_dc = 2 * NA_MAX_KH - 1, 2 * NA_KW - 1
    i = np.arange(NA_RB)[:, None]
    jr = np.arange(NA_WIN)[None, :]
    row_sel = np.zeros((3, NA_RB, NA_WIN, n_dr), np.float32)
    for rt in range(3):
        off = (0, -NA_MAX_KH // 2, NA_RB - NA_WIN)[rt]
        if rt == 0:
            rs = np.maximum(i - NA_MAX_KH // 2, 0)
        elif rt == 1:
            rs = i - NA_MAX_KH // 2
        else:
            rs = np.minimum(i - NA_MAX_KH // 2, NA_RB - NA_MAX_KH)
        krow = off + jr
        ii, jj = np.nonzero((krow >= rs) & (krow < rs + NA_MAX_KH))
        row_sel[rt, ii, jj, (krow - i)[ii, jj] + NA_MAX_KH - 1] = 1.0
    qc = np.arange(NA_QCB)[:, None]
    kc = np.arange(NA_KCB)[None, :]
    col_sel = np.zeros((NA_NCB, NA_QCB, NA_KCB, n_dc), np.float32)
    for n in range(NA_NCB):
        qcol = n * NA_QCB + qc
        kcol = _NA_KB[n] + kc
        cs = np.clip(qcol - NA_KW // 2, 0, GRID_W - NA_KW)
        qq, kk = np.nonzero((kcol >= cs) & (kcol < cs + NA_KW))
        col_sel[n, qq, kk, (kcol - qcol)[qq, kk] + NA_KW - 1] = 1.0
    ok = (np.einsum("xijr->xij", row_sel)[:, None, :, None, :, None]
          * np.einsum("nqkc->nqk", col_sel)[None, :, None, :, None, :]) > 0
    hi = lax.Precision.HIGHEST
    r = rpb.astype(jnp.float32).reshape(NA_HEADS // NA_HPB, NA_HPB, n_dr, n_dc)
    a = jnp.einsum("ghrc,xijr->gxhijc", r, row_sel, precision=hi)
    t = jnp.einsum("gxhijc,nqkc->gxnhiqjk", a, col_sel, precision=hi)
    t = jnp.where(ok[None, :, :, None], t * LOG2E, NEG)
    return t.reshape(NA_HEADS // NA_HPB, 3 * NA_NCB, NA_HPB * NA_MQ, NA_NK)


def _natten_kernel(q_ref, *refs):
    kv_refs, (z_ref, tab_ref, o_ref, acc_s) = refs[:-4], refs[-4:]
    nrb = pl.num_programs(2) * NA_BPS
    tq = NA_RB * GRID_W
    lane = lax.broadcasted_iota(jnp.int32, (NA_MQ, NA_LANES), 1) // NA_HEAD_DIM

    def keys(xf, n):
        kb = _NA_KB[n]
        return jnp.concatenate([xf[j * GRID_W + kb:j * GRID_W + kb + NA_KCB] for j in range(NA_WIN)],
                               axis=0).astype(jnp.bfloat16)

    for blk in range(NA_BPS):
        rb = pl.program_id(2) * NA_BPS + blk
        rt = jnp.where(rb == 0, 0, jnp.where(rb == nrb - 1, 2, 1))
        kf = kv_refs[2 * blk][0].astype(jnp.float32)
        vf = kv_refs[2 * blk + 1][0].astype(jnp.float32)
        q = q_ref[0, blk * tq:(blk + 1) * tq, :]
        for n in range(NA_NCB):
            qn = jnp.concatenate([q[i * GRID_W + n * NA_QCB:i * GRID_W + (n + 1) * NA_QCB] for i in range(NA_RB)],
                                 axis=0)
            qs = jnp.concatenate([jnp.where(lane == hh, qn, 0.0).astype(jnp.bfloat16) for hh in range(NA_HPB)],
                                 axis=0)
            s = _dot_nt(qs, keys(kf, n)) + tab_ref[0, rt * NA_NCB + n]
            m = jnp.max(s, axis=1, keepdims=True)
            e = jnp.exp2(s - m)
            l = jnp.sum(e, axis=1, keepdims=True)
            o = _dot(e.astype(jnp.bfloat16), keys(vf, n)) / l
            on = jnp.zeros((NA_MQ, NA_LANES), jnp.float32)
            for hh in range(NA_HPB):
                on = jnp.where(lane == hh, o[hh * NA_MQ:(hh + 1) * NA_MQ], on)
            for i in range(NA_RB):
                r0 = blk * tq + i * GRID_W + n * NA_QCB
                acc_s[r0:r0 + NA_QCB, :] = on[i * NA_QCB:(i + 1) * NA_QCB]
    o_ref[0] = (acc_s[...] * _silu(z_ref[0].astype(jnp.float32))).astype(o_ref.dtype)


def _natten(proj, table):
    B, T, _ = proj.shape
    rows = T // GRID_W
    assert T % GRID_W == 0 and rows % (NA_RB * NA_BPS) == 0 and rows >= NA_WIN
    ng = NA_HEADS // NA_HPB
    tq = NA_BPS * NA_RB * GRID_W
    cpt = D_MODEL // NA_LANES

    def win(col, blk):
        def index(g, b, s):
            start = jnp.clip((s * NA_BPS + blk) * NA_RB - NA_MAX_KH // 2, 0, rows - NA_WIN)
            return (b, start * GRID_W, (col * cpt + g) * NA_LANES)
        return pl.BlockSpec((pl.Element(1), pl.Element(NA_WIN * GRID_W), pl.Element(NA_LANES)), index)

    windows = []
    for blk in range(NA_BPS):
        windows += [win(COL_NK, blk), win(COL_NV, blk)]
    return pl.pallas_call(
        _natten_kernel,
        grid=(ng, B, rows // (NA_RB * NA_BPS)),
        in_specs=[pl.BlockSpec((1, tq, NA_LANES), lambda g, b, s: (b, s, COL_NQ * cpt + g))]
        + windows
        + [pl.BlockSpec((1, tq, NA_LANES), lambda g, b, s: (b, s, COL_NZ * cpt + g)),
           pl.BlockSpec((1, 3 * NA_NCB, NA_HPB * NA_MQ, NA_NK), lambda g, b, s: (g, 0, 0, 0))],
        out_specs=pl.BlockSpec((1, tq, NA_LANES), lambda g, b, s: (b, s, g)),
        out_shape=jax.ShapeDtypeStruct((B, T, NA_WIDTH), jnp.bfloat16),
        scratch_shapes=[pltpu.VMEM((tq, NA_LANES), jnp.float32)],
        compiler_params=pltpu.CompilerParams(
            dimension_semantics=("arbitrary", "arbitrary", "arbitrary"),
            vmem_limit_bytes=VMEM_LIMIT_BYTES),
        name="natten",
    )(proj, *([proj] * (2 * NA_BPS)), proj, table)


def _out_kernel(x_ref, hf_ref, hb_ref, o_ref, z_ref, yb_ref, ga_ref, gb_ref, mhw_ref, wda_ref, wdb_ref, wo_ref,
                fnw_ref, y_ref):
    f32 = jnp.float32
    tm = x_ref.shape[1]
    for r0 in range(0, tm, OUT_SUB):
        rows = slice(r0, r0 + OUT_SUB)
        h = hf_ref[0, :, rows].astype(f32) + hb_ref[0, :, rows].astype(f32)
        h = h.T * _sigmoid(o_ref[0, rows, :].astype(f32))
        parts = []
        for hd in range(M_HEADS):
            hh = h[:, hd * M_HEAD_DIM:(hd + 1) * M_HEAD_DIM]
            parts.append(hh * lax.rsqrt(jnp.mean(hh * hh, axis=-1, keepdims=True) + EPS))
        hn = jnp.concatenate(parts, axis=-1) * mhw_ref[...]
        ya = (hn * _silu(z_ref[0, rows, :].astype(f32))).astype(jnp.bfloat16)
        merged = (_sigmoid(ga_ref[0, rows, :].astype(f32)) * _dot(ya, wda_ref[...])
                  + _sigmoid(gb_ref[0, rows, :].astype(f32)) * _dot(yb_ref[0, rows, :], wdb_ref[...]))
        y = x_ref[0, rows, :] + _dot(merged.astype(jnp.bfloat16), wo_ref[...])
        y = y * lax.rsqrt(jnp.mean(y * y, axis=-1, keepdims=True) + EPS)
        y_ref[0, rows, :] = y * fnw_ref[...]


def _out(x, h_f, h_b, proj, y_b, mh_w, w_da, w_db, w_o, fn_w):
    B, T, D = x.shape
    tm = min(OUT_TM, T)
    assert T % tm == 0

    def tok(col=0):
        return pl.BlockSpec((1, tm, D_MODEL), lambda b, i, col=col: (b, i, col))

    def feat():
        return pl.BlockSpec((1, M_WIDTH, tm), lambda b, i: (b, 0, i))

    def const(shape):
        return pl.BlockSpec(shape, lambda b, i: (0,) * len(shape))

    return pl.pallas_call(
        _out_kernel,
        grid=(B, T // tm),
        in_specs=[tok(), feat(), feat(), tok(COL_MO), tok(COL_MZ), tok(), tok(COL_GA), tok(COL_GB),
                  const((1, D)), const((D, D)), const((D, D)), const((D, D)), const((1, D))],
        out_specs=tok(),
        out_shape=jax.ShapeDtypeStruct((B, T, D), jnp.float32),
        compiler_params=pltpu.CompilerParams(
            dimension_semantics=("arbitrary", "arbitrary"),
            vmem_limit_bytes=VMEM_LIMIT_BYTES),
        name="out",
    )(x, h_f, h_b, proj, proj, y_b, proj, proj, mh_w, w_da, w_db, w_o, fn_w)


def _prepare(norm_w, w_in, b_gate, conv_w, conv_b, mh_norm_w, rpb, w_down_a, w_down_b, w_out, final_norm_w):
    bf16 = jnp.bfloat16
    g0 = 5 * M_WIDTH
    v0 = 2 * M_WIDTH
    nq0 = g0 + N_GATES
    w_steps = jnp.concatenate([w_in[:, v0:v0 + M_WIDTH].T, w_in[:, :v0], w_in[:, v0 + M_WIDTH:g0],
                               w_in[:, nq0:nq0 + NA_WIDTH] * (LOG2E * NA_HEAD_DIM ** -0.5),
                               w_in[:, nq0 + NA_WIDTH:]], axis=1).astype(bf16)
    w_gt = w_in[:, g0:g0 + N_GATES].T.astype(bf16)
    cw = jnp.pad(conv_w, ((0, SUBLANES - CONV_W), (0, 0)))
    return dict(
        norm_w=norm_w.reshape(1, D_MODEL), w_steps=w_steps, w_gt=w_gt, b_gt=b_gate.reshape(N_GATES, 1),
        conv_w=cw, conv_b=conv_b.reshape(1, 2 * M_WIDTH), mh_w=mh_norm_w.reshape(1, M_WIDTH),
        table=_na_bias_table(rpb), w_da=w_down_a.astype(bf16), w_db=w_down_b.astype(bf16),
        w_o=w_out.astype(bf16), fn_w=final_norm_w.reshape(1, D_MODEL))


def _trunk(x, p):
    proj, v_t, g_t = _proj(x, p["norm_w"], p["w_steps"], p["w_gt"], p["b_gt"], p["conv_w"], p["conv_b"])
    h_f, h_b = _mlstm(proj, v_t, _gates(g_t, min(MLSTM_L, x.shape[1])))
    y_b = _natten(proj, p["table"])
    return _out(x, h_f, h_b, proj, y_b, p["mh_w"], p["w_da"], p["w_db"], p["w_o"], p["fn_w"])


def kernel(x_prompt, x_sample, norm_w, w_in, b_gate, conv_w, conv_b, mh_norm_w, rpb, w_down_a, w_down_b, w_out,
           final_norm_w):
    assert norm_w.shape[0] == 1, "single-layer trunk"
    p = _prepare(norm_w[0], w_in[0], b_gate[0], conv_w[0], conv_b[0], mh_norm_w[0], rpb[0],
                 w_down_a[0], w_down_b[0], w_out[0], final_norm_w)
    return (_trunk(x_prompt, p), _trunk(x_sample, p))
```

```python
import functools

import numpy as np
import jax
import jax.numpy as jnp
from jax import lax
from jax.experimental import pallas as pl
from jax.experimental.pallas import tpu as pltpu

D_MODEL = 1024
GRID_W = 64
M_HEADS = 4
M_HEAD_DIM = 256
M_WIDTH = M_HEADS * M_HEAD_DIM
CONV_W = 5
N_GATES = 4 * M_HEADS
NA_HEADS = 16
NA_HEAD_DIM = 64
NA_WIDTH = NA_HEADS * NA_HEAD_DIM
NA_MAX_KH = 8
NA_KW = 16
NA_QCB = 16
NA_KCB = 32
EPS = 1e-6
NEG = -1e30
LOG2E = 1.4426950408889634

LANES = 128
SUBLANES = 8
BF16_SUBLANES = 16
VMEM_LIMIT_BYTES = 56 * 1024 * 1024

STEP_MV, STEP_MQ, STEP_MK = 0, 1, 2
N_STEPS = 11
COL_MQ, COL_MK, COL_MO, COL_MZ, COL_NQ, COL_NK, COL_NV, COL_NZ, COL_GA, COL_GB = range(10)
D_MAIN = 10 * D_MODEL

PROJ_TM = 2048
HALO = BF16_SUBLANES
CONV_PIECE = 512
NORM_PIECE = 512
MLSTM_L = 256
GATES_TT = 2048
M_AUG = M_HEAD_DIM + BF16_SUBLANES
NA_RB = 4
NA_WIN = NA_RB + NA_MAX_KH
NA_BPS = 8
NA_HPB = 4
NA_LANES = NA_HPB * NA_HEAD_DIM
OUT_TM = 512
OUT_SUB = 256


def _dot(a, b):
    return jnp.dot(a, b, preferred_element_type=jnp.float32)


def _dot_nt(a, b):
    return lax.dot_general(a, b, (((1,), (1,)), ((), ())), preferred_element_type=jnp.float32)


def _sigmoid(x):
    return 1.0 / (1.0 + jnp.exp2(x * -LOG2E))


def _silu(x):
    return x / (1.0 + jnp.exp2(x * -LOG2E))


def _log_sigmoid(x):
    return jnp.minimum(x, 0.0) - jnp.log1p(jnp.exp(-jnp.abs(x)))


def _proj_kernel(x_ref, xp_ref, xn_ref_, nw_ref, w_ref, wgt_ref, bgt_ref, cw_ref, cb_ref,
                 o_ref, vt_ref, gt_ref, xn_s, acc_s):
    i = pl.program_id(1)
    j = pl.program_id(2)
    ni = pl.num_programs(1)
    tm = x_ref.shape[1]
    pad = CONV_W // 2

    def norm(x):
        y = x * lax.rsqrt(jnp.mean(x * x, axis=-1, keepdims=True) + EPS)
        return y * nw_ref[...]

    @pl.when(j == STEP_MV)
    def _():
        prev = jnp.where(i > 0, norm(xp_ref[0]), 0.0).astype(jnp.bfloat16)
        nxt = jnp.where(i < ni - 1, norm(xn_ref_[0]), 0.0).astype(jnp.bfloat16)
        xn_s[0:HALO, :] = prev
        xn_s[HALO + tm:2 * HALO + tm, :] = nxt
        for r0 in range(0, tm, NORM_PIECE):
            cur = norm(x_ref[0, r0:r0 + NORM_PIECE, :]).astype(jnp.bfloat16)
            xn_s[HALO + r0:HALO + r0 + NORM_PIECE, :] = cur
            gt_ref[0, :, r0:r0 + NORM_PIECE] = _dot_nt(wgt_ref[...], cur) + bgt_ref[...]
            vt = _dot_nt(w_ref[...], cur).astype(vt_ref.dtype)
            for cc in range(NORM_PIECE // MLSTM_L):
                vt_ref[0, r0 // MLSTM_L + cc] = vt[:, cc * MLSTM_L:(cc + 1) * MLSTM_L]

    @pl.when((j == STEP_MQ) | (j == STEP_MK))
    def _():
        scale = jnp.where(j == STEP_MK, M_HEAD_DIM ** -0.5, 1.0).astype(o_ref.dtype)
        for c in range(tm // CONV_PIECE):
            r0 = c * CONV_PIECE
            acc_s[c % 2] = _dot(xn_s[r0:r0 + CONV_PIECE + 2 * HALO, :], w_ref[...])
            y = cb_ref[...]
            for t in range(CONV_W):
                y = y + acc_s[c % 2, pl.ds(HALO - pad + t, CONV_PIECE), :] * cw_ref[t:t + 1, :]
            o_ref[0, r0:r0 + CONV_PIECE, :] = _silu(y).astype(o_ref.dtype) * scale

    @pl.when(j > STEP_MK)
    def _():
        o_ref[0] = _dot(xn_s[HALO:HALO + tm, :], w_ref[...]).astype(o_ref.dtype)


def _proj(x, norm_w, w_steps, w_gt, b_gt, conv_w, conv_b):
    B, T, D = x.shape
    tm = min(PROJ_TM, T)
    assert T % tm == 0 and tm % CONV_PIECE == 0 and tm % NORM_PIECE == 0 and NORM_PIECE % MLSTM_L == 0
    ni = T // tm
    hb = tm // HALO
    return pl.pallas_call(
        _proj_kernel,
        grid=(B, ni, N_STEPS),
        in_specs=[
            pl.BlockSpec((1, tm, D), lambda b, i, j: (b, i, 0)),
            pl.BlockSpec((1, HALO, D), lambda b, i, j: (b, jnp.maximum(i * hb - 1, 0), 0)),
            pl.BlockSpec((1, HALO, D), lambda b, i, j: (b, jnp.minimum((i + 1) * hb, T // HALO - 1), 0)),
            pl.BlockSpec((1, D), lambda b, i, j: (0, 0)),
            pl.BlockSpec((D, D_MODEL), lambda b, i, j: (0, j)),
            pl.BlockSpec((N_GATES, D), lambda b, i, j: (0, 0)),
            pl.BlockSpec((N_GATES, 1), lambda b, i, j: (0, 0)),
            pl.BlockSpec((SUBLANES, D_MODEL), lambda b, i, j: (0, jnp.clip(j - STEP_MQ, 0, 1))),
            pl.BlockSpec((1, D_MODEL), lambda b, i, j: (0, jnp.clip(j - STEP_MQ, 0, 1))),
        ],
        out_specs=[
            pl.BlockSpec((1, tm, D_MODEL), lambda b, i, j: (b, i, jnp.maximum(j - STEP_MQ, 0))),
            pl.BlockSpec((1, tm // MLSTM_L, M_WIDTH, MLSTM_L), lambda b, i, j: (b, i, 0, 0)),
            pl.BlockSpec((1, N_GATES, tm), lambda b, i, j: (b, 0, i)),
        ],
        out_shape=[
            jax.ShapeDtypeStruct((B, T, D_MAIN), jnp.bfloat16),
            jax.ShapeDtypeStruct((B, T // MLSTM_L, M_WIDTH, MLSTM_L), jnp.bfloat16),
            jax.ShapeDtypeStruct((B, N_GATES, T), jnp.float32),
        ],
        scratch_shapes=[
            pltpu.VMEM((tm + 2 * HALO, D), jnp.bfloat16),
            pltpu.VMEM((2, CONV_PIECE + 2 * HALO, D_MODEL), jnp.float32),
        ],
        compiler_params=pltpu.CompilerParams(
            dimension_semantics=("arbitrary", "arbitrary", "arbitrary"),
            vmem_limit_bytes=VMEM_LIMIT_BYTES),
        name="proj",
    )(x, x, x, norm_w, w_steps, w_gt, b_gt, conv_w, conv_b)


def _scan_lanes(x, pos, L, d, op, fill):
    n = x.shape[1]
    sh = 1
    while sh < L:
        if d == 0:
            x = op(x, jnp.where(pos >= sh, pltpu.roll(x, sh, axis=1), fill))
        else:
            x = op(x, jnp.where(pos < L - sh, pltpu.roll(x, n - sh, axis=1), fill))
        sh *= 2
    return x


def _gates_kernel(L, gt_ref, abc_ref, at_ref):
    n = gt_ref.shape[2]
    pos = lax.broadcasted_iota(jnp.int32, (SUBLANES, n), 1) % L
    a_rows = []
    for d in range(2):
        rows = slice(d * SUBLANES, (d + 1) * SUBLANES)
        g8 = gt_ref[0, rows, :]
        b8 = _scan_lanes(_log_sigmoid(g8), pos, L, d, jnp.add, 0.0)
        b8 = pltpu.roll(b8, M_HEADS, axis=0)
        a8 = g8 - b8
        abc_ref[0, d * SUBLANES:(d + 1) * SUBLANES, :] = a8
        abc_ref[0, N_GATES + d * SUBLANES:N_GATES + (d + 1) * SUBLANES, :] = b8
        abc_ref[0, 2 * N_GATES + d * SUBLANES:2 * N_GATES + (d + 1) * SUBLANES, :] = _scan_lanes(
            a8, pos, L, d, jnp.maximum, -jnp.inf)
        a_rows.append(a8)
    pad = jnp.zeros((LANES - 2 * SUBLANES, n), jnp.float32)
    at_ref[0] = jnp.concatenate(a_rows + [pad], axis=0).T


def _gates(g_t, L):
    B, _, T = g_t.shape
    tg = min(GATES_TT, T)
    assert T % tg == 0 and tg % L == 0
    return pl.pallas_call(
        functools.partial(_gates_kernel, L),
        grid=(B, T // tg),
        in_specs=[pl.BlockSpec((1, N_GATES, tg), lambda b, i: (b, 0, i))],
        out_specs=[pl.BlockSpec((1, 3 * N_GATES, tg), lambda b, i: (b, 0, i)),
                   pl.BlockSpec((1, tg, LANES), lambda b, i: (b, i, 0))],
        out_shape=[jax.ShapeDtypeStruct((B, 3 * N_GATES, T), jnp.float32),
                   jax.ShapeDtypeStruct((B, T, LANES), jnp.float32)],
        compiler_params=pltpu.CompilerParams(
            dimension_semantics=("arbitrary", "arbitrary"),
            vmem_limit_bytes=VMEM_LIMIT_BYTES),
        name="gates",
    )(g_t)


def _mlstm_kernel(qf_ref, kf_ref, vtf_ref, abcf_ref, atf_ref, qb_ref, kb_ref, vtb_ref, abcb_ref, atb_ref,
                  hf_ref, hb_ref, st_ref, m_ref):
    L = qf_ref.shape[1]
    bf16 = jnp.bfloat16

    @pl.when(pl.program_id(1) == 0)
    def _():
        st_ref[...] = jnp.zeros_like(st_ref)
        m_ref[...] = jnp.zeros_like(m_ref)

    row = lax.broadcasted_iota(jnp.int32, (L, L), 0)
    col = lax.broadcasted_iota(jnp.int32, (L, L), 1)
    ones_rows = jnp.ones((BF16_SUBLANES, L), bf16)
    dirs = ((qf_ref, kf_ref, vtf_ref, abcf_ref, atf_ref, hf_ref),
            (qb_ref, kb_ref, vtb_ref, abcb_ref, atb_ref, hb_ref))
    pre = []
    for d, (_, _, _, abc_ref, _, _) in enumerate(dirs):
        last = L - 1 if d == 0 else 0
        a8, b8, cm8 = (abc_ref[0, kind * N_GATES + d * SUBLANES:kind * N_GATES + (d + 1) * SUBLANES, :]
                       for kind in range(3))
        m8 = m_ref[d]
        c8 = jnp.maximum(cm8, m8)
        c_last = c8[:, last:last + 1]
        m_ref[d] = b8[:, last:last + 1] + c_last
        pre.append(dict(
            c=c8, w=jnp.exp(m8 - c8), floor=jnp.exp(-b8 - c8), wk=jnp.exp(a8 - c_last).astype(bf16),
            decay=jnp.exp(m8 - c_last),
            mask=(row <= col) if d == 0 else (row >= col)))

    for h in range(M_HEADS):
        for d, (q_ref, k_ref, vt_ref, _, at_ref, ht_ref) in enumerate(dirs):
            p = pre[d]
            idx = d * M_HEADS + h
            lanes = slice(h * M_HEAD_DIM, (h + 1) * M_HEAD_DIM)
            q = q_ref[0, :, lanes]
            k = k_ref[0, :, lanes]
            vt = jnp.concatenate([vt_ref[0, 0, lanes, :], ones_rows], axis=0)
            st = st_ref[idx]
            a_col = at_ref[0, :, d * SUBLANES + h:d * SUBLANES + h + 1]

            e = jnp.exp(jnp.where(p["mask"], a_col - p["c"][h:h + 1, :], -jnp.inf))
            pt = (_dot_nt(k, q) * e).astype(bf16)
            num = _dot(vt, pt) + p["w"][h:h + 1, :] * _dot_nt(st.astype(bf16), q)
            den = num[M_HEAD_DIM:M_HEAD_DIM + 1, :]
            ht_ref[0, 0, lanes, :] = (num[:M_HEAD_DIM]
                                      / jnp.maximum(jnp.abs(den), p["floor"][h:h + 1, :])).astype(ht_ref.dtype)

    for h in range(M_HEADS):
        for d, (_, k_ref, vt_ref, _, _, _) in enumerate(dirs):
            p = pre[d]
            idx = d * M_HEADS + h
            lanes = slice(h * M_HEAD_DIM, (h + 1) * M_HEAD_DIM)
            vt = jnp.concatenate([vt_ref[0, 0, lanes, :], ones_rows], axis=0)
            st_ref[idx] = (p["decay"][h:h + 1, :] * st_ref[idx]
                           + _dot(vt * p["wk"][h:h + 1, :], k_ref[0, :, lanes]))


def _mlstm(proj, v_t, gates):
    B, T, _ = proj.shape
    L = MLSTM_L
    assert T % L == 0
    nc = T // L
    abc, a_t = gates

    def tok(width, col, rev):
        return pl.BlockSpec((1, L, width), lambda b, c: (b, nc - 1 - c if rev else c, col))

    def feat(rows, rev):
        return pl.BlockSpec((1, rows, L), lambda b, c: (b, 0, nc - 1 - c if rev else c))

    def chunk(rev):
        return pl.BlockSpec((1, 1, M_WIDTH, L), lambda b, c: (b, nc - 1 - c if rev else c, 0, 0))

    def side(rev):
        return [tok(M_WIDTH, COL_MQ, rev), tok(M_WIDTH, COL_MK, rev), chunk(rev),
                feat(3 * N_GATES, rev), tok(LANES, 0, rev)]

    return pl.pallas_call(
        _mlstm_kernel,
        grid=(B, nc),
        in_specs=side(False) + side(True),
        out_specs=[chunk(False), chunk(True)],
        out_shape=[
            jax.ShapeDtypeStruct((B, nc, M_WIDTH, L), jnp.bfloat16),
            jax.ShapeDtypeStruct((B, nc, M_WIDTH, L), jnp.bfloat16),
        ],
        scratch_shapes=[
            pltpu.VMEM((2 * M_HEADS, M_AUG, M_HEAD_DIM), jnp.float32),
            pltpu.VMEM((2, SUBLANES, 1), jnp.float32),
        ],
        compiler_params=pltpu.CompilerParams(
            dimension_semantics=("arbitrary", "arbitrary"),
            vmem_limit_bytes=VMEM_LIMIT_BYTES),
        name="mlstm",
    )(proj, proj, v_t, abc, a_t, proj, proj, v_t, abc, a_t)


_NA_KB = tuple(int(v) for v in np.clip(np.arange(GRID_W // NA_QCB) * NA_QCB - NA_KW // 2, 0, GRID_W - NA_KCB))
NA_NCB = GRID_W // NA_QCB
NA_MQ = NA_RB * NA_QCB
NA_NK = NA_WIN * NA_KCB


def _na_bias_table(rpb):
    n_dr, n_dc = 2 * NA_MAX_KH - 1, 2 * NA_KW - 1
    i = np.arange(NA_RB)[:, None]
    jr = np.arange(NA_WIN)[None, :]
    row_sel = np.zeros((3, NA_RB, NA_WIN, n_dr), np.float32)
    for rt in range(3):
        off = (0, -NA_MAX_KH // 2, NA_RB - NA_WIN)[rt]
        if rt == 0:
            rs = np.maximum(i - NA_MAX_KH // 2, 0)
        elif rt == 1:
            rs = i - NA_MAX_KH // 2
        else:
            rs = np.minimum(i - NA_MAX_KH // 2, NA_RB - NA_MAX_KH)
        krow = off + jr
        ii, jj = np.nonzero((krow >= rs) & (krow < rs + NA_MAX_KH))
        row_sel[rt, ii, jj, (krow - i)[ii, jj] + NA_MAX_KH - 1] = 1.0
    qc = np.arange(NA_QCB)[:, None]
    kc = np.arange(NA_KCB)[None, :]
    col_sel = np.zeros((NA_NCB, NA_QCB, NA_KCB, n_dc), np.float32)
    for n in range(NA_NCB):
        qcol = n * NA_QCB + qc
        kcol = _NA_KB[n] + kc
        cs = np.clip(qcol - NA_KW // 2, 0, GRID_W - NA_KW)
        qq, kk = np.nonzero((kcol >= cs) & (kcol < cs + NA_KW))
        col_sel[n, qq, kk, (kcol - qcol)[qq, kk] + NA_KW - 1] = 1.0
    ok = (np.einsum("xijr->xij", row_sel)[:, None, :, None, :, None]
          * np.einsum("nqkc->nqk", col_sel)[None, :, None, :, None, :]) > 0
    hi = lax.Precision.HIGHEST
    r = rpb.astype(jnp.float32).reshape(NA_HEADS // NA_HPB, NA_HPB, n_dr, n_dc)
    a = jnp.einsum("ghrc,xijr->gxhijc", r, row_sel, precision=hi)
    t = jnp.einsum("gxhijc,nqkc->gxnhiqjk", a, col_sel, precision=hi)
    t = jnp.where(ok[None, :, :, None], t * LOG2E, NEG)
    return t.reshape(NA_HEADS // NA_HPB, 3 * NA_NCB, NA_HPB * NA_MQ, NA_NK)


def _natten_kernel(q_ref, *refs):
    kv_refs, (z_ref, tab_ref, o_ref, acc_s) = refs[:-4], refs[-4:]
    nrb = pl.num_programs(2) * NA_BPS
    tq = NA_RB * GRID_W
    lane = lax.broadcasted_iota(jnp.int32, (NA_MQ, NA_LANES), 1) // NA_HEAD_DIM

    def keys(xf, n):
        kb = _NA_KB[n]
        return jnp.concatenate([xf[j * GRID_W + kb:j * GRID_W + kb + NA_KCB] for j in range(NA_WIN)],
                               axis=0).astype(jnp.bfloat16)

    for blk in range(NA_BPS):
        rb = pl.program_id(2) * NA_BPS + blk
        rt = jnp.where(rb == 0, 0, jnp.where(rb == nrb - 1, 2, 1))
        kf = kv_refs[2 * blk][0].astype(jnp.float32)
        vf = kv_refs[2 * blk + 1][0].astype(jnp.float32)
        q = q_ref[0, blk * tq:(blk + 1) * tq, :]
        for n in range(NA_NCB):
            qn = jnp.concatenate([q[i * GRID_W + n * NA_QCB:i * GRID_W + (n + 1) * NA_QCB] for i in range(NA_RB)],
                                 axis=0)
            qs = jnp.concatenate([jnp.where(lane == hh, qn, 0.0).astype(jnp.bfloat16) for hh in range(NA_HPB)],
                                 axis=0)
            s = _dot_nt(qs, keys(kf, n)) + tab_ref[0, rt * NA_NCB + n]
            m = jnp.max(s, axis=1, keepdims=True)
            e = jnp.exp2(s - m)
            l = jnp.sum(e, axis=1, keepdims=True)
            o = _dot(e.astype(jnp.bfloat16), keys(vf, n)) / l
            on = jnp.zeros((NA_MQ, NA_LANES), jnp.float32)
            for hh in range(NA_HPB):
                on = jnp.where(lane == hh, o[hh * NA_MQ:(hh + 1) * NA_MQ], on)
            for i in range(NA_RB):
                r0 = blk * tq + i * GRID_W + n * NA_QCB
                acc_s[r0:r0 + NA_QCB, :] = on[i * NA_QCB:(i + 1) * NA_QCB]
    o_ref[0] = (acc_s[...] * _silu(z_ref[0].astype(jnp.float32))).astype(o_ref.dtype)


def _natten(proj, table):
    B, T, _ = proj.shape
    rows = T // GRID_W
    assert T % GRID_W == 0 and rows % (NA_RB * NA_BPS) == 0 and rows >= NA_WIN
    ng = NA_HEADS // NA_HPB
    tq = NA_BPS * NA_RB * GRID_W
    cpt = D_MODEL // NA_LANES

    def win(col, blk):
        def index(g, b, s):
            start = jnp.clip((s * NA_BPS + blk) * NA_RB - NA_MAX_KH // 2, 0, rows - NA_WIN)
            return (b, start * GRID_W, (col * cpt + g) * NA_LANES)
        return pl.BlockSpec((pl.Element(1), pl.Element(NA_WIN * GRID_W), pl.Element(NA_LANES)), index)

    windows = []
    for blk in range(NA_BPS):
        windows += [win(COL_NK, blk), win(COL_NV, blk)]
    return pl.pallas_call(
        _natten_kernel,
        grid=(ng, B, rows // (NA_RB * NA_BPS)),
        in_specs=[pl.BlockSpec((1, tq, NA_LANES), lambda g, b, s: (b, s, COL_NQ * cpt + g))]
        + windows
        + [pl.BlockSpec((1, tq, NA_LANES), lambda g, b, s: (b, s, COL_NZ * cpt + g)),
           pl.BlockSpec((1, 3 * NA_NCB, NA_HPB * NA_MQ, NA_NK), lambda g, b, s: (g, 0, 0, 0))],
        out_specs=pl.BlockSpec((1, tq, NA_LANES), lambda g, b, s: (b, s, g)),
        out_shape=jax.ShapeDtypeStruct((B, T, NA_WIDTH), jnp.bfloat16),
        scratch_shapes=[pltpu.VMEM((tq, NA_LANES), jnp.float32)],
        compiler_params=pltpu.CompilerParams(
            dimension_semantics=("arbitrary", "arbitrary", "arbitrary"),
            vmem_limit_bytes=VMEM_LIMIT_BYTES),
        name="natten",
    )(proj, *([proj] * (2 * NA_BPS)), proj, table)


def _out_kernel(x_ref, hf_ref, hb_ref, o_ref, z_ref, yb_ref, ga_ref, gb_ref, mhw_ref, wda_ref, wdb_ref, wo_ref,
                fnw_ref, y_ref):
    f32 = jnp.float32
    tm = x_ref.shape[1]
    for r0 in range(0, tm, OUT_SUB):
        rows = slice(r0, r0 + OUT_SUB)
        h = hf_ref[0, r0 // OUT_SUB].astype(f32) + hb_ref[0, r0 // OUT_SUB].astype(f32)
        h = h.T * _sigmoid(o_ref[0, rows, :].astype(f32))
        parts = []
        for hd in range(M_HEADS):
            hh = h[:, hd * M_HEAD_DIM:(hd + 1) * M_HEAD_DIM]
            parts.append(hh * lax.rsqrt(jnp.mean(hh * hh, axis=-1, keepdims=True) + EPS))
        hn = jnp.concatenate(parts, axis=-1) * mhw_ref[...]
        ya = (hn * _silu(z_ref[0, rows, :].astype(f32))).astype(jnp.bfloat16)
        merged = (_sigmoid(ga_ref[0, rows, :].astype(f32)) * _dot(ya, wda_ref[...])
                  + _sigmoid(gb_ref[0, rows, :].astype(f32)) * _dot(yb_ref[0, rows, :], wdb_ref[...]))
        y = x_ref[0, rows, :] + _dot(merged.astype(jnp.bfloat16), wo_ref[...])
        y = y * lax.rsqrt(jnp.mean(y * y, axis=-1, keepdims=True) + EPS)
        y_ref[0, rows, :] = y * fnw_ref[...]


def _out(x, h_f, h_b, proj, y_b, mh_w, w_da, w_db, w_o, fn_w):
    B, T, D = x.shape
    tm = min(OUT_TM, T)
    assert T % tm == 0 and OUT_SUB == MLSTM_L

    def tok(col=0):
        return pl.BlockSpec((1, tm, D_MODEL), lambda b, i, col=col: (b, i, col))

    def feat():
        return pl.BlockSpec((1, tm // MLSTM_L, M_WIDTH, MLSTM_L), lambda b, i: (b, i, 0, 0))

    def const(shape):
        return pl.BlockSpec(shape, lambda b, i: (0,) * len(shape))

    return pl.pallas_call(
        _out_kernel,
        grid=(B, T // tm),
        in_specs=[tok(), feat(), feat(), tok(COL_MO), tok(COL_MZ), tok(), tok(COL_GA), tok(COL_GB),
                  const((1, D)), const((D, D)), const((D, D)), const((D, D)), const((1, D))],
        out_specs=tok(),
        out_shape=jax.ShapeDtypeStruct((B, T, D), jnp.float32),
        compiler_params=pltpu.CompilerParams(
            dimension_semantics=("arbitrary", "arbitrary"),
            vmem_limit_bytes=VMEM_LIMIT_BYTES),
        name="out",
    )(x, h_f, h_b, proj, proj, y_b, proj, proj, mh_w, w_da, w_db, w_o, fn_w)


def _prepare(norm_w, w_in, b_gate, conv_w, conv_b, mh_norm_w, rpb, w_down_a, w_down_b, w_out, final_norm_w):
    bf16 = jnp.bfloat16
    g0 = 5 * M_WIDTH
    v0 = 2 * M_WIDTH
    nq0 = g0 + N_GATES
    w_steps = jnp.concatenate([w_in[:, v0:v0 + M_WIDTH].T, w_in[:, :v0], w_in[:, v0 + M_WIDTH:g0],
                               w_in[:, nq0:nq0 + NA_WIDTH] * (LOG2E * NA_HEAD_DIM ** -0.5),
                               w_in[:, nq0 + NA_WIDTH:]], axis=1).astype(bf16)
    w_gt = w_in[:, g0:g0 + N_GATES].T.astype(bf16)
    cw = jnp.pad(conv_w, ((0, SUBLANES - CONV_W), (0, 0)))
    return dict(
        norm_w=norm_w.reshape(1, D_MODEL), w_steps=w_steps, w_gt=w_gt, b_gt=b_gate.reshape(N_GATES, 1),
        conv_w=cw, conv_b=conv_b.reshape(1, 2 * M_WIDTH), mh_w=mh_norm_w.reshape(1, M_WIDTH),
        table=_na_bias_table(rpb), w_da=w_down_a.astype(bf16), w_db=w_down_b.astype(bf16),
        w_o=w_out.astype(bf16), fn_w=final_norm_w.reshape(1, D_MODEL))


def _trunk(x, p):
    proj, v_t, g_t = _proj(x, p["norm_w"], p["w_steps"], p["w_gt"], p["b_gt"], p["conv_w"], p["conv_b"])
    h_f, h_b = _mlstm(proj, v_t, _gates(g_t, min(MLSTM_L, x.shape[1])))
    y_b = _natten(proj, p["table"])
    return _out(x, h_f, h_b, proj, y_b, p["mh_w"], p["w_da"], p["w_db"], p["w_o"], p["fn_w"])


def kernel(x_prompt, x_sample, norm_w, w_in, b_gate, conv_w, conv_b, mh_norm_w, rpb, w_down_a, w_down_b, w_out,
           final_norm_w):
    assert norm_w.shape[0] == 1, "single-layer trunk"
    p = _prepare(norm_w[0], w_in[0], b_gate[0], conv_w[0], conv_b[0], mh_norm_w[0], rpb[0],
                 w_down_a[0], w_down_b[0], w_out[0], final_norm_w)
    return (_trunk(x_prompt, p), _trunk(x_sample, p))
```

```python
import functools

import numpy as np
import jax
import jax.numpy as jnp
from jax import lax
from jax.experimental import pallas as pl
from jax.experimental.pallas import tpu as pltpu

D_MODEL = 1024
GRID_W = 64
M_HEADS = 4
M_HEAD_DIM = 256
M_WIDTH = M_HEADS * M_HEAD_DIM
CONV_W = 5
N_GATES = 4 * M_HEADS
NA_HEADS = 16
NA_HEAD_DIM = 64
NA_WIDTH = NA_HEADS * NA_HEAD_DIM
NA_MAX_KH = 8
NA_KW = 16
NA_QCB = 16
NA_KCB = 32
EPS = 1e-6
NEG = -1e30
LOG2E = 1.4426950408889634

LANES = 128
SUBLANES = 8
BF16_SUBLANES = 16
VMEM_LIMIT_BYTES = 56 * 1024 * 1024

STEP_MV, STEP_MQ, STEP_MK = 0, 9, 10
N_STEPS = 11
COL_MO, COL_MZ, COL_NQ, COL_NK, COL_NV, COL_NZ, COL_GA, COL_GB, COL_MQ, COL_MK = range(10)
D_MAIN = 10 * D_MODEL

PROJ_TM = 2048
HALO = BF16_SUBLANES
CONV_PIECE = 512
NORM_PIECE = 512
MLSTM_L = 256
GATES_TT = 2048
M_AUG = M_HEAD_DIM + BF16_SUBLANES
NA_RB = 4
NA_WIN = NA_RB + NA_MAX_KH
NA_BPS = 8
NA_HPB = 4
NA_LANES = NA_HPB * NA_HEAD_DIM
OUT_TM = 512
OUT_SUB = 256


def _dot(a, b):
    return jnp.dot(a, b, preferred_element_type=jnp.float32)


def _dot_nt(a, b):
    return lax.dot_general(a, b, (((1,), (1,)), ((), ())), preferred_element_type=jnp.float32)


def _sigmoid(x):
    return 1.0 / (1.0 + jnp.exp2(x * -LOG2E))


def _silu(x):
    return x / (1.0 + jnp.exp2(x * -LOG2E))


def _log_sigmoid(x):
    return jnp.minimum(x, 0.0) - jnp.log1p(jnp.exp(-jnp.abs(x)))


def _proj_kernel(x_ref, xp_ref, xn_ref_, nw_ref, w_ref, wgt_ref, bgt_ref, cw_ref, cb_ref,
                 o_ref, vt_ref, gt_ref, xn_s, acc_s):
    i = pl.program_id(1)
    j = pl.program_id(2)
    ni = pl.num_programs(1)
    tm = x_ref.shape[1]
    pad = CONV_W // 2

    def norm(x):
        y = x * lax.rsqrt(jnp.mean(x * x, axis=-1, keepdims=True) + EPS)
        return y * nw_ref[...]

    @pl.when(j == STEP_MV)
    def _():
        prev = jnp.where(i > 0, norm(xp_ref[0]), 0.0).astype(jnp.bfloat16)
        nxt = jnp.where(i < ni - 1, norm(xn_ref_[0]), 0.0).astype(jnp.bfloat16)
        xn_s[0:HALO, :] = prev
        xn_s[HALO + tm:2 * HALO + tm, :] = nxt
        for r0 in range(0, tm, NORM_PIECE):
            cur = norm(x_ref[0, r0:r0 + NORM_PIECE, :]).astype(jnp.bfloat16)
            xn_s[HALO + r0:HALO + r0 + NORM_PIECE, :] = cur
            gt_ref[0, :, r0:r0 + NORM_PIECE] = _dot_nt(wgt_ref[...], cur) + bgt_ref[...]
            vt = _dot_nt(w_ref[...], cur).astype(vt_ref.dtype)
            for cc in range(NORM_PIECE // MLSTM_L):
                vt_ref[0, r0 // MLSTM_L + cc] = vt[:, cc * MLSTM_L:(cc + 1) * MLSTM_L]

    @pl.when((j == STEP_MQ) | (j == STEP_MK))
    def _():
        scale = jnp.where(j == STEP_MK, M_HEAD_DIM ** -0.5, 1.0).astype(o_ref.dtype)
        for c in range(tm // CONV_PIECE):
            r0 = c * CONV_PIECE
            acc_s[c % 2] = _dot(xn_s[r0:r0 + CONV_PIECE + 2 * HALO, :], w_ref[...])
            y = cb_ref[...]
            for t in range(CONV_W):
                y = y + acc_s[c % 2, pl.ds(HALO - pad + t, CONV_PIECE), :] * cw_ref[t:t + 1, :]
            o_ref[0, r0:r0 + CONV_PIECE, :] = _silu(y).astype(o_ref.dtype) * scale

    @pl.when((j > STEP_MV) & (j < STEP_MQ))
    def _():
        o_ref[0] = _dot(xn_s[HALO:HALO + tm, :], w_ref[...]).astype(o_ref.dtype)


def _proj(x, norm_w, w_steps, w_gt, b_gt, conv_w, conv_b):
    B, T, D = x.shape
    tm = min(PROJ_TM, T)
    assert T % tm == 0 and tm % CONV_PIECE == 0 and tm % NORM_PIECE == 0 and NORM_PIECE % MLSTM_L == 0
    ni = T // tm
    hb = tm // HALO
    return pl.pallas_call(
        _proj_kernel,
        grid=(B, ni, N_STEPS),
        in_specs=[
            pl.BlockSpec((1, tm, D), lambda b, i, j: (b, i, 0)),
            pl.BlockSpec((1, HALO, D), lambda b, i, j: (b, jnp.maximum(i * hb - 1, 0), 0)),
            pl.BlockSpec((1, HALO, D), lambda b, i, j: (b, jnp.minimum((i + 1) * hb, T // HALO - 1), 0)),
            pl.BlockSpec((1, D), lambda b, i, j: (0, 0)),
            pl.BlockSpec((D, D_MODEL), lambda b, i, j: (0, j)),
            pl.BlockSpec((N_GATES, D), lambda b, i, j: (0, 0)),
            pl.BlockSpec((N_GATES, 1), lambda b, i, j: (0, 0)),
            pl.BlockSpec((SUBLANES, D_MODEL), lambda b, i, j: (0, jnp.clip(j - STEP_MQ, 0, 1))),
            pl.BlockSpec((1, D_MODEL), lambda b, i, j: (0, jnp.clip(j - STEP_MQ, 0, 1))),
        ],
        out_specs=[
            pl.BlockSpec((1, tm, D_MODEL), lambda b, i, j: (b, i, jnp.maximum(j - 1, 0))),
            pl.BlockSpec((1, tm // MLSTM_L, M_WIDTH, MLSTM_L), lambda b, i, j: (b, i, 0, 0)),
            pl.BlockSpec((1, N_GATES, tm), lambda b, i, j: (b, 0, i)),
        ],
        out_shape=[
            jax.ShapeDtypeStruct((B, T, D_MAIN), jnp.bfloat16),
            jax.ShapeDtypeStruct((B, T // MLSTM_L, M_WIDTH, MLSTM_L), jnp.bfloat16),
            jax.ShapeDtypeStruct((B, N_GATES, T), jnp.float32),
        ],
        scratch_shapes=[
            pltpu.VMEM((tm + 2 * HALO, D), jnp.bfloat16),
            pltpu.VMEM((2, CONV_PIECE + 2 * HALO, D_MODEL), jnp.float32),
        ],
        compiler_params=pltpu.CompilerParams(
            dimension_semantics=("arbitrary", "arbitrary", "arbitrary"),
            vmem_limit_bytes=VMEM_LIMIT_BYTES),
        name="proj",
    )(x, x, x, norm_w, w_steps, w_gt, b_gt, conv_w, conv_b)


def _scan_lanes(x, pos, L, d, op, fill):
    n = x.shape[1]
    sh = 1
    while sh < L:
        if d == 0:
            x = op(x, jnp.where(pos >= sh, pltpu.roll(x, sh, axis=1), fill))
        else:
            x = op(x, jnp.where(pos < L - sh, pltpu.roll(x, n - sh, axis=1), fill))
        sh *= 2
    return x


def _gates_kernel(L, gt_ref, abc_ref, at_ref):
    n = gt_ref.shape[2]
    pos = lax.broadcasted_iota(jnp.int32, (SUBLANES, n), 1) % L
    a_rows = []
    for d in range(2):
        rows = slice(d * SUBLANES, (d + 1) * SUBLANES)
        g8 = gt_ref[0, rows, :]
        b8 = _scan_lanes(_log_sigmoid(g8), pos, L, d, jnp.add, 0.0)
        b8 = pltpu.roll(b8, M_HEADS, axis=0)
        a8 = g8 - b8
        abc_ref[0, d * SUBLANES:(d + 1) * SUBLANES, :] = a8
        abc_ref[0, N_GATES + d * SUBLANES:N_GATES + (d + 1) * SUBLANES, :] = b8
        abc_ref[0, 2 * N_GATES + d * SUBLANES:2 * N_GATES + (d + 1) * SUBLANES, :] = _scan_lanes(
            a8, pos, L, d, jnp.maximum, -jnp.inf)
        a_rows.append(a8)
    pad = jnp.zeros((LANES - 2 * SUBLANES, n), jnp.float32)
    at_ref[0] = jnp.concatenate(a_rows + [pad], axis=0).T


def _gates(g_t, L):
    B, _, T = g_t.shape
    tg = min(GATES_TT, T)
    assert T % tg == 0 and tg % L == 0
    return pl.pallas_call(
        functools.partial(_gates_kernel, L),
        grid=(B, T // tg),
        in_specs=[pl.BlockSpec((1, N_GATES, tg), lambda b, i: (b, 0, i))],
        out_specs=[pl.BlockSpec((1, 3 * N_GATES, tg), lambda b, i: (b, 0, i)),
                   pl.BlockSpec((1, tg, LANES), lambda b, i: (b, i, 0))],
        out_shape=[jax.ShapeDtypeStruct((B, 3 * N_GATES, T), jnp.float32),
                   jax.ShapeDtypeStruct((B, T, LANES), jnp.float32)],
        compiler_params=pltpu.CompilerParams(
            dimension_semantics=("arbitrary", "arbitrary"),
            vmem_limit_bytes=VMEM_LIMIT_BYTES),
        name="gates",
    )(g_t)


def _mlstm_kernel(qf_ref, kf_ref, vtf_ref, abcf_ref, atf_ref, qb_ref, kb_ref, vtb_ref, abcb_ref, atb_ref,
                  hf_ref, hb_ref, st_ref, m_ref):
    L = qf_ref.shape[1]
    bf16 = jnp.bfloat16

    @pl.when(pl.program_id(1) == 0)
    def _():
        st_ref[...] = jnp.zeros_like(st_ref)
        m_ref[...] = jnp.zeros_like(m_ref)

    row = lax.broadcasted_iota(jnp.int32, (L, L), 0)
    col = lax.broadcasted_iota(jnp.int32, (L, L), 1)
    ones_rows = jnp.ones((BF16_SUBLANES, L), bf16)
    dirs = ((qf_ref, kf_ref, vtf_ref, abcf_ref, atf_ref, hf_ref),
            (qb_ref, kb_ref, vtb_ref, abcb_ref, atb_ref, hb_ref))
    pre = []
    for d, (_, _, _, abc_ref, _, _) in enumerate(dirs):
        last = L - 1 if d == 0 else 0
        a8, b8, cm8 = (abc_ref[0, kind * N_GATES + d * SUBLANES:kind * N_GATES + (d + 1) * SUBLANES, :]
                       for kind in range(3))
        m8 = m_ref[d]
        c8 = jnp.maximum(cm8, m8)
        c_last = c8[:, last:last + 1]
        m_ref[d] = b8[:, last:last + 1] + c_last
        pre.append(dict(
            c=c8, w=jnp.exp(m8 - c8), floor=jnp.exp(-b8 - c8), wk=jnp.exp(a8 - c_last).astype(bf16),
            decay=jnp.exp(m8 - c_last),
            mask=(row <= col) if d == 0 else (row >= col)))

    for h in range(M_HEADS):
        for d, (q_ref, k_ref, vt_ref, _, at_ref, ht_ref) in enumerate(dirs):
            p = pre[d]
            idx = d * M_HEADS + h
            lanes = slice(h * M_HEAD_DIM, (h + 1) * M_HEAD_DIM)
            q = q_ref[0, :, lanes]
            k = k_ref[0, :, lanes]
            vt = jnp.concatenate([vt_ref[0, 0, lanes, :], ones_rows], axis=0)
            st = st_ref[idx]
            a_col = at_ref[0, :, d * SUBLANES + h:d * SUBLANES + h + 1]

            e = jnp.exp(jnp.where(p["mask"], a_col - p["c"][h:h + 1, :], -jnp.inf))
            pt = (_dot_nt(k, q) * e).astype(bf16)
            num = _dot(vt, pt) + p["w"][h:h + 1, :] * _dot_nt(st.astype(bf16), q)
            den = num[M_HEAD_DIM:M_HEAD_DIM + 1, :]
            ht_ref[0, 0, lanes, :] = (num[:M_HEAD_DIM]
                                      / jnp.maximum(jnp.abs(den), p["floor"][h:h + 1, :])).astype(ht_ref.dtype)

    for h in range(M_HEADS):
        for d, (_, k_ref, vt_ref, _, _, _) in enumerate(dirs):
            p = pre[d]
            idx = d * M_HEADS + h
            lanes = slice(h * M_HEAD_DIM, (h + 1) * M_HEAD_DIM)
            vt = jnp.concatenate([vt_ref[0, 0, lanes, :], ones_rows], axis=0)
            st_ref[idx] = (p["decay"][h:h + 1, :] * st_ref[idx]
                           + _dot(vt * p["wk"][h:h + 1, :], k_ref[0, :, lanes]))


def _mlstm(proj, v_t, gates):
    B, T, _ = proj.shape
    L = MLSTM_L
    assert T % L == 0
    nc = T // L
    abc, a_t = gates

    def tok(width, col, rev):
        return pl.BlockSpec((1, L, width), lambda b, c: (b, nc - 1 - c if rev else c, col))

    def feat(rows, rev):
        return pl.BlockSpec((1, rows, L), lambda b, c: (b, 0, nc - 1 - c if rev else c))

    def chunk(rev):
        return pl.BlockSpec((1, 1, M_WIDTH, L), lambda b, c: (b, nc - 1 - c if rev else c, 0, 0))

    def side(rev):
        return [tok(M_WIDTH, COL_MQ, rev), tok(M_WIDTH, COL_MK, rev), chunk(rev),
                feat(3 * N_GATES, rev), tok(LANES, 0, rev)]

    return pl.pallas_call(
        _mlstm_kernel,
        grid=(B, nc),
        in_specs=side(False) + side(True),
        out_specs=[chunk(False), chunk(True)],
        out_shape=[
            jax.ShapeDtypeStruct((B, nc, M_WIDTH, L), jnp.bfloat16),
            jax.ShapeDtypeStruct((B, nc, M_WIDTH, L), jnp.bfloat16),
        ],
        scratch_shapes=[
            pltpu.VMEM((2 * M_HEADS, M_AUG, M_HEAD_DIM), jnp.float32),
            pltpu.VMEM((2, SUBLANES, 1), jnp.float32),
        ],
        compiler_params=pltpu.CompilerParams(
            dimension_semantics=("arbitrary", "arbitrary"),
            vmem_limit_bytes=VMEM_LIMIT_BYTES),
        name="mlstm",
    )(proj, proj, v_t, abc, a_t, proj, proj, v_t, abc, a_t)


_NA_KB = tuple(int(v) for v in np.clip(np.arange(GRID_W // NA_QCB) * NA_QCB - NA_KW // 2, 0, GRID_W - NA_KCB))
NA_NCB = GRID_W // NA_QCB
NA_MQ = NA_RB * NA_QCB
NA_NK = NA_WIN * NA_KCB


def _na_bias_table(rpb):
    n_dr, n_dc = 2 * NA_MAX_KH - 1, 2 * NA_KW - 1
    i = np.arange(NA_RB)[:, None]
    jr = np.arange(NA_WIN)[None, :]
    row_sel = np.zeros((3, NA_RB, NA_WIN, n_dr + 1), np.float32)
    for rt in range(3):
        off = (0, -NA_MAX_KH // 2, NA_RB - NA_WIN)[rt]
        if rt == 0:
            rs = np.maximum(i - NA_MAX_KH // 2, 0)
        elif rt == 1:
            rs = i - NA_MAX_KH // 2
        else:
            rs = np.minimum(i - NA_MAX_KH // 2, NA_RB - NA_MAX_KH)
        krow = off + jr
        ii, jj = np.nonzero((krow >= rs) & (krow < rs + NA_MAX_KH))
        row_sel[rt, ii, jj, (krow - i)[ii, jj] + NA_MAX_KH - 1] = 1.0
    row_sel[..., n_dr] = 1.0 - row_sel.sum(-1)
    qc = np.arange(NA_QCB)[:, None]
    kc = np.arange(NA_KCB)[None, :]
    col_sel = np.zeros((NA_NCB, NA_QCB, NA_KCB, n_dc + 1), np.float32)
    for n in range(NA_NCB):
        qcol = n * NA_QCB + qc
        kcol = _NA_KB[n] + kc
        cs = np.clip(qcol - NA_KW // 2, 0, GRID_W - NA_KW)
        qq, kk = np.nonzero((kcol >= cs) & (kcol < cs + NA_KW))
        col_sel[n, qq, kk, (kcol - qcol)[qq, kk] + NA_KW - 1] = 1.0
    col_sel[..., n_dc] = 1.0 - col_sel.sum(-1)
    r = jnp.pad(rpb.astype(jnp.float32) * LOG2E, ((0, 0), (0, 1), (0, 1)), constant_values=NEG)
    r = r.reshape(NA_HEADS // NA_HPB, NA_HPB, n_dr + 1, n_dc + 1)
    hi = lax.Precision.HIGHEST
    a = jnp.einsum("ghrc,xijr->gxhijc", r, row_sel, precision=hi)
    t = jnp.einsum("gxhijc,nqkc->gxnhiqjk", a, col_sel, precision=hi)
    return t.reshape(NA_HEADS // NA_HPB, 3 * NA_NCB, NA_HPB * NA_MQ, NA_NK)


def _natten_kernel(q_ref, *refs):
    kv_refs, (z_ref, tab_ref, o_ref, acc_s) = refs[:-4], refs[-4:]
    nrb = pl.num_programs(2) * NA_BPS
    tq = NA_RB * GRID_W
    lane = lax.broadcasted_iota(jnp.int32, (NA_MQ, NA_LANES), 1) // NA_HEAD_DIM

    def keys(xf, n):
        kb = _NA_KB[n]
        return jnp.concatenate([xf[j * GRID_W + kb:j * GRID_W + kb + NA_KCB] for j in range(NA_WIN)],
                               axis=0).astype(jnp.bfloat16)

    for blk in range(NA_BPS):
        rb = pl.program_id(2) * NA_BPS + blk
        rt = jnp.where(rb == 0, 0, jnp.where(rb == nrb - 1, 2, 1))
        kf = kv_refs[2 * blk][0].astype(jnp.float32)
        vf = kv_refs[2 * blk + 1][0].astype(jnp.float32)
        q = q_ref[0, blk * tq:(blk + 1) * tq, :]
        for n in range(NA_NCB):
            qn = jnp.concatenate([q[i * GRID_W + n * NA_QCB:i * GRID_W + (n + 1) * NA_QCB] for i in range(NA_RB)],
                                 axis=0)
            qs = jnp.concatenate([jnp.where(lane == hh, qn, 0.0).astype(jnp.bfloat16) for hh in range(NA_HPB)],
                                 axis=0)
            s = _dot_nt(qs, keys(kf, n)) + tab_ref[0, rt * NA_NCB + n]
            m = jnp.max(s, axis=1, keepdims=True)
            e = jnp.exp2(s - m)
            l = jnp.sum(e, axis=1, keepdims=True)
            o = _dot(e.astype(jnp.bfloat16), keys(vf, n)) / l
            on = jnp.zeros((NA_MQ, NA_LANES), jnp.float32)
            for hh in range(NA_HPB):
                on = jnp.where(lane == hh, o[hh * NA_MQ:(hh + 1) * NA_MQ], on)
            for i in range(NA_RB):
                r0 = blk * tq + i * GRID_W + n * NA_QCB
                acc_s[r0:r0 + NA_QCB, :] = on[i * NA_QCB:(i + 1) * NA_QCB]
    o_ref[0] = (acc_s[...] * _silu(z_ref[0].astype(jnp.float32))).astype(o_ref.dtype)


def _natten(proj, table):
    B, T, _ = proj.shape
    rows = T // GRID_W
    assert T % GRID_W == 0 and rows % (NA_RB * NA_BPS) == 0 and rows >= NA_WIN
    ng = NA_HEADS // NA_HPB
    tq = NA_BPS * NA_RB * GRID_W
    cpt = D_MODEL // NA_LANES

    def win(col, blk):
        def index(g, b, s):
            start = jnp.clip((s * NA_BPS + blk) * NA_RB - NA_MAX_KH // 2, 0, rows - NA_WIN)
            return (b, start * GRID_W, (col * cpt + g) * NA_LANES)
        return pl.BlockSpec((pl.Element(1), pl.Element(NA_WIN * GRID_W), pl.Element(NA_LANES)), index)

    windows = []
    for blk in range(NA_BPS):
        windows += [win(COL_NK, blk), win(COL_NV, blk)]
    return pl.pallas_call(
        _natten_kernel,
        grid=(ng, B, rows // (NA_RB * NA_BPS)),
        in_specs=[pl.BlockSpec((1, tq, NA_LANES), lambda g, b, s: (b, s, COL_NQ * cpt + g))]
        + windows
        + [pl.BlockSpec((1, tq, NA_LANES), lambda g, b, s: (b, s, COL_NZ * cpt + g)),
           pl.BlockSpec((1, 3 * NA_NCB, NA_HPB * NA_MQ, NA_NK), lambda g, b, s: (g, 0, 0, 0))],
        out_specs=pl.BlockSpec((1, tq, NA_LANES), lambda g, b, s: (b, s, g)),
        out_shape=jax.ShapeDtypeStruct((B, T, NA_WIDTH), jnp.bfloat16),
        scratch_shapes=[pltpu.VMEM((tq, NA_LANES), jnp.float32)],
        compiler_params=pltpu.CompilerParams(
            dimension_semantics=("arbitrary", "arbitrary", "arbitrary"),
            vmem_limit_bytes=VMEM_LIMIT_BYTES),
        name="natten",
    )(proj, *([proj] * (2 * NA_BPS)), proj, table)


def _out_kernel(x_ref, hf_ref, hb_ref, o_ref, z_ref, yb_ref, ga_ref, gb_ref, mhw_ref, wda_ref, wdb_ref, wo_ref,
                fnw_ref, y_ref):
    f32 = jnp.float32
    tm = x_ref.shape[1]
    for r0 in range(0, tm, OUT_SUB):
        rows = slice(r0, r0 + OUT_SUB)
        h = hf_ref[0, r0 // OUT_SUB].astype(f32) + hb_ref[0, r0 // OUT_SUB].astype(f32)
        h = h.T * _sigmoid(o_ref[0, rows, :].astype(f32))
        parts = []
        for hd in range(M_HEADS):
            hh = h[:, hd * M_HEAD_DIM:(hd + 1) * M_HEAD_DIM]
            parts.append(hh * lax.rsqrt(jnp.mean(hh * hh, axis=-1, keepdims=True) + EPS))
        hn = jnp.concatenate(parts, axis=-1) * mhw_ref[...]
        ya = (hn * _silu(z_ref[0, rows, :].astype(f32))).astype(jnp.bfloat16)
        merged = (_sigmoid(ga_ref[0, rows, :].astype(f32)) * _dot(ya, wda_ref[...])
                  + _sigmoid(gb_ref[0, rows, :].astype(f32)) * _dot(yb_ref[0, rows, :], wdb_ref[...]))
        y = x_ref[0, rows, :] + _dot(merged.astype(jnp.bfloat16), wo_ref[...])
        y = y * lax.rsqrt(jnp.mean(y * y, axis=-1, keepdims=True) + EPS)
        y_ref[0, rows, :] = y * fnw_ref[...]


def _out(x, h_f, h_b, proj, y_b, mh_w, w_da, w_db, w_o, fn_w):
    B, T, D = x.shape
    tm = min(OUT_TM, T)
    assert T % tm == 0 and OUT_SUB == MLSTM_L

    def tok(col=0):
        return pl.BlockSpec((1, tm, D_MODEL), lambda b, i, col=col: (b, i, col))

    def feat():
        return pl.BlockSpec((1, tm // MLSTM_L, M_WIDTH, MLSTM_L), lambda b, i: (b, i, 0, 0))

    def const(shape):
        return pl.BlockSpec(shape, lambda b, i: (0,) * len(shape))

    return pl.pallas_call(
        _out_kernel,
        grid=(B, T // tm),
        in_specs=[tok(), feat(), feat(), tok(COL_MO), tok(COL_MZ), tok(), tok(COL_GA), tok(COL_GB),
                  const((1, D)), const((D, D)), const((D, D)), const((D, D)), const((1, D))],
        out_specs=tok(),
        out_shape=jax.ShapeDtypeStruct((B, T, D), jnp.float32),
        compiler_params=pltpu.CompilerParams(
            dimension_semantics=("arbitrary", "arbitrary"),
            vmem_limit_bytes=VMEM_LIMIT_BYTES),
        name="out",
    )(x, h_f, h_b, proj, proj, y_b, proj, proj, mh_w, w_da, w_db, w_o, fn_w)


def _prepare(norm_w, w_in, b_gate, conv_w, conv_b, mh_norm_w, rpb, w_down_a, w_down_b, w_out, final_norm_w):
    bf16 = jnp.bfloat16
    g0 = 5 * M_WIDTH
    v0 = 2 * M_WIDTH
    nq0 = g0 + N_GATES
    w_steps = jnp.concatenate([w_in[:, v0:v0 + M_WIDTH].T, w_in[:, v0 + M_WIDTH:g0],
                               w_in[:, nq0:nq0 + NA_WIDTH] * (LOG2E * NA_HEAD_DIM ** -0.5),
                               w_in[:, nq0 + NA_WIDTH:], w_in[:, :v0]], axis=1).astype(bf16)
    w_gt = w_in[:, g0:g0 + N_GATES].T.astype(bf16)
    cw = jnp.pad(conv_w, ((0, SUBLANES - CONV_W), (0, 0)))
    return dict(
        norm_w=norm_w.reshape(1, D_MODEL), w_steps=w_steps, w_gt=w_gt, b_gt=b_gate.reshape(N_GATES, 1),
        conv_w=cw, conv_b=conv_b.reshape(1, 2 * M_WIDTH), mh_w=mh_norm_w.reshape(1, M_WIDTH),
        table=_na_bias_table(rpb), w_da=w_down_a.astype(bf16), w_db=w_down_b.astype(bf16),
        w_o=w_out.astype(bf16), fn_w=final_norm_w.reshape(1, D_MODEL))


def _trunk(x, p):
    proj, v_t, g_t = _proj(x, p["norm_w"], p["w_steps"], p["w_gt"], p["b_gt"], p["conv_w"], p["conv_b"])
    h_f, h_b = _mlstm(proj, v_t, _gates(g_t, min(MLSTM_L, x.shape[1])))
    y_b = _natten(proj, p["table"])
    return _out(x, h_f, h_b, proj, y_b, p["mh_w"], p["w_da"], p["w_db"], p["w_o"], p["fn_w"])


def kernel(x_prompt, x_sample, norm_w, w_in, b_gate, conv_w, conv_b, mh_norm_w, rpb, w_down_a, w_down_b, w_out,
           final_norm_w):
    assert norm_w.shape[0] == 1, "single-layer trunk"
    p = _prepare(norm_w[0], w_in[0], b_gate[0], conv_w[0], conv_b[0], mh_norm_w[0], rpb[0],
                 w_down_a[0], w_down_b[0], w_out[0], final_norm_w)
    return (_trunk(x_prompt, p), _trunk(x_sample, p))
```

```python
import functools

import numpy as np
import jax
import jax.numpy as jnp
from jax import lax
from jax.experimental import pallas as pl
from jax.experimental.pallas import tpu as pltpu

D_MODEL = 1024
GRID_W = 64
M_HEADS = 4
M_HEAD_DIM = 256
M_WIDTH = M_HEADS * M_HEAD_DIM
CONV_W = 5
N_GATES = 4 * M_HEADS
NA_HEADS = 16
NA_HEAD_DIM = 64
NA_WIDTH = NA_HEADS * NA_HEAD_DIM
NA_MAX_KH = 8
NA_KW = 16
NA_QCB = 16
NA_KCB = 32
EPS = 1e-6
NEG = -1e30
LOG2E = 1.4426950408889634

LANES = 128
SUBLANES = 8
BF16_SUBLANES = 16
VMEM_LIMIT_BYTES = 56 * 1024 * 1024

STEP_MV, STEP_MQ, STEP_MK = 0, 9, 10
N_STEPS = 11
COL_MO, COL_MZ, COL_NQ, COL_NK, COL_NV, COL_NZ, COL_GA, COL_GB, COL_MQ, COL_MK = range(10)
D_MAIN = 10 * D_MODEL

PROJ_TM = 2048
HALO = BF16_SUBLANES
CONV_PIECE = 512
NORM_PIECE = 512
MLSTM_L = 256
MLSTM_SEQ_T = 2048
GATES_TT = 2048
M_AUG = M_HEAD_DIM + BF16_SUBLANES
NA_RB = 4
NA_WIN = NA_RB + NA_MAX_KH
NA_BPS = 8
NA_HPB = 4
NA_LANES = NA_HPB * NA_HEAD_DIM
OUT_TM = 512
OUT_SUB = 256


def _dot(a, b):
    return jnp.dot(a, b, preferred_element_type=jnp.float32)


def _dot_nt(a, b):
    return lax.dot_general(a, b, (((1,), (1,)), ((), ())), preferred_element_type=jnp.float32)


def _sigmoid(x):
    return 1.0 / (1.0 + jnp.exp2(x * -LOG2E))


def _silu(x):
    return x / (1.0 + jnp.exp2(x * -LOG2E))


def _log_sigmoid(x):
    return jnp.minimum(x, 0.0) - jnp.log1p(jnp.exp(-jnp.abs(x)))


def _proj_kernel(x_ref, xp_ref, xn_ref_, nw_ref, w_ref, wgt_ref, bgt_ref, cw_ref, cb_ref,
                 o_ref, vt_ref, gt_ref, xn_s, acc_s):
    i = pl.program_id(1)
    j = pl.program_id(2)
    ni = pl.num_programs(1)
    tm = x_ref.shape[1]
    pad = CONV_W // 2

    def norm(x):
        y = x * lax.rsqrt(jnp.mean(x * x, axis=-1, keepdims=True) + EPS)
        return y * nw_ref[...]

    @pl.when(j == STEP_MV)
    def _():
        prev = jnp.where(i > 0, norm(xp_ref[0]), 0.0).astype(jnp.bfloat16)
        nxt = jnp.where(i < ni - 1, norm(xn_ref_[0]), 0.0).astype(jnp.bfloat16)
        xn_s[0:HALO, :] = prev
        xn_s[HALO + tm:2 * HALO + tm, :] = nxt
        for r0 in range(0, tm, NORM_PIECE):
            cur = norm(x_ref[0, r0:r0 + NORM_PIECE, :]).astype(jnp.bfloat16)
            xn_s[HALO + r0:HALO + r0 + NORM_PIECE, :] = cur
            gt_ref[0, :, r0:r0 + NORM_PIECE] = _dot_nt(wgt_ref[...], cur) + bgt_ref[...]
            vt = _dot_nt(w_ref[...], cur).astype(vt_ref.dtype)
            for cc in range(NORM_PIECE // MLSTM_L):
                vt_ref[0, r0 // MLSTM_L + cc] = vt[:, cc * MLSTM_L:(cc + 1) * MLSTM_L]

    @pl.when((j == STEP_MQ) | (j == STEP_MK))
    def _():
        scale = jnp.where(j == STEP_MK, M_HEAD_DIM ** -0.5, 1.0).astype(o_ref.dtype)
        for c in range(tm // CONV_PIECE):
            r0 = c * CONV_PIECE
            acc_s[c % 2] = _dot(xn_s[r0:r0 + CONV_PIECE + 2 * HALO, :], w_ref[...])
            y = cb_ref[...]
            for t in range(CONV_W):
                y = y + acc_s[c % 2, pl.ds(HALO - pad + t, CONV_PIECE), :] * cw_ref[t:t + 1, :]
            o_ref[0, r0:r0 + CONV_PIECE, :] = _silu(y).astype(o_ref.dtype) * scale

    @pl.when((j > STEP_MV) & (j < STEP_MQ))
    def _():
        o_ref[0] = _dot(xn_s[HALO:HALO + tm, :], w_ref[...]).astype(o_ref.dtype)


def _proj(x, norm_w, w_steps, w_gt, b_gt, conv_w, conv_b):
    B, T, D = x.shape
    tm = min(PROJ_TM, T)
    assert T % tm == 0 and tm % CONV_PIECE == 0 and tm % NORM_PIECE == 0 and NORM_PIECE % MLSTM_L == 0
    ni = T // tm
    hb = tm // HALO
    return pl.pallas_call(
        _proj_kernel,
        grid=(B, ni, N_STEPS),
        in_specs=[
            pl.BlockSpec((1, tm, D), lambda b, i, j: (b, i, 0)),
            pl.BlockSpec((1, HALO, D), lambda b, i, j: (b, jnp.maximum(i * hb - 1, 0), 0)),
            pl.BlockSpec((1, HALO, D), lambda b, i, j: (b, jnp.minimum((i + 1) * hb, T // HALO - 1), 0)),
            pl.BlockSpec((1, D), lambda b, i, j: (0, 0)),
            pl.BlockSpec((D, D_MODEL), lambda b, i, j: (0, j)),
            pl.BlockSpec((N_GATES, D), lambda b, i, j: (0, 0)),
            pl.BlockSpec((N_GATES, 1), lambda b, i, j: (0, 0)),
            pl.BlockSpec((SUBLANES, D_MODEL), lambda b, i, j: (0, jnp.clip(j - STEP_MQ, 0, 1))),
            pl.BlockSpec((1, D_MODEL), lambda b, i, j: (0, jnp.clip(j - STEP_MQ, 0, 1))),
        ],
        out_specs=[
            pl.BlockSpec((1, tm, D_MODEL), lambda b, i, j: (b, i, jnp.maximum(j - 1, 0))),
            pl.BlockSpec((1, tm // MLSTM_L, M_WIDTH, MLSTM_L), lambda b, i, j: (b, i, 0, 0)),
            pl.BlockSpec((1, N_GATES, tm), lambda b, i, j: (b, 0, i)),
        ],
        out_shape=[
            jax.ShapeDtypeStruct((B, T, D_MAIN), jnp.bfloat16),
            jax.ShapeDtypeStruct((B, T // MLSTM_L, M_WIDTH, MLSTM_L), jnp.bfloat16),
            jax.ShapeDtypeStruct((B, N_GATES, T), jnp.float32),
        ],
        scratch_shapes=[
            pltpu.VMEM((tm + 2 * HALO, D), jnp.bfloat16),
            pltpu.VMEM((2, CONV_PIECE + 2 * HALO, D_MODEL), jnp.float32),
        ],
        compiler_params=pltpu.CompilerParams(
            dimension_semantics=("arbitrary", "arbitrary", "arbitrary"),
            vmem_limit_bytes=VMEM_LIMIT_BYTES),
        name="proj",
    )(x, x, x, norm_w, w_steps, w_gt, b_gt, conv_w, conv_b)


def _scan_lanes(x, pos, L, d, op, fill):
    n = x.shape[1]
    sh = 1
    while sh < L:
        if d == 0:
            x = op(x, jnp.where(pos >= sh, pltpu.roll(x, sh, axis=1), fill))
        else:
            x = op(x, jnp.where(pos < L - sh, pltpu.roll(x, n - sh, axis=1), fill))
        sh *= 2
    return x


def _gates_kernel(L, gt_ref, abc_ref, at_ref):
    n = gt_ref.shape[2]
    pos = lax.broadcasted_iota(jnp.int32, (SUBLANES, n), 1) % L
    a_rows = []
    for d in range(2):
        rows = slice(d * SUBLANES, (d + 1) * SUBLANES)
        g8 = gt_ref[0, rows, :]
        b8 = _scan_lanes(_log_sigmoid(g8), pos, L, d, jnp.add, 0.0)
        b8 = pltpu.roll(b8, M_HEADS, axis=0)
        a8 = g8 - b8
        cm8 = _scan_lanes(a8, pos, L, d, jnp.maximum, -jnp.inf)
        for kind, val in enumerate((a8, b8, cm8)):
            for cc in range(n // L):
                abc_ref[0, cc, kind * N_GATES + d * SUBLANES:kind * N_GATES + (d + 1) * SUBLANES, :] = (
                    val[:, cc * L:(cc + 1) * L])
        a_rows.append(a8)
    pad = jnp.zeros((LANES - 2 * SUBLANES, n), jnp.float32)
    at_ref[0] = jnp.concatenate(a_rows + [pad], axis=0).T


def _gates(g_t, L):
    B, _, T = g_t.shape
    tg = min(GATES_TT, T)
    assert T % tg == 0 and tg % L == 0
    return pl.pallas_call(
        functools.partial(_gates_kernel, L),
        grid=(B, T // tg),
        in_specs=[pl.BlockSpec((1, N_GATES, tg), lambda b, i: (b, 0, i))],
        out_specs=[pl.BlockSpec((1, tg // L, 3 * N_GATES, L), lambda b, i: (b, i, 0, 0)),
                   pl.BlockSpec((1, tg, LANES), lambda b, i: (b, i, 0))],
        out_shape=[jax.ShapeDtypeStruct((B, T // L, 3 * N_GATES, L), jnp.float32),
                   jax.ShapeDtypeStruct((B, T, LANES), jnp.float32)],
        compiler_params=pltpu.CompilerParams(
            dimension_semantics=("arbitrary", "arbitrary"),
            vmem_limit_bytes=VMEM_LIMIT_BYTES),
        name="gates",
    )(g_t)


class _ChunkView:
    def __init__(self, d, q_ref, k_ref, vt_ref, abc_ref, at_ref, ht_ref, row0, chunk):
        self.d, self.refs, self.row0, self.chunk = d, (q_ref, k_ref, vt_ref, abc_ref, at_ref, ht_ref), row0, chunk

    def _rows(self):
        return pl.ds(self.row0, MLSTM_L)

    def q(self, lanes):
        return self.refs[0][0, self._rows(), lanes]

    def k(self, lanes):
        return self.refs[1][0, self._rows(), lanes]

    def vt(self, lanes):
        return self.refs[2][0, self.chunk, lanes, :]

    def gate(self, kind):
        r0 = kind * N_GATES + self.d * SUBLANES
        return self.refs[3][0, self.chunk, r0:r0 + SUBLANES, :]

    def a_col(self, h):
        return self.refs[4][0, self._rows(), self.d * SUBLANES + h:self.d * SUBLANES + h + 1]

    def put_h(self, lanes, value):
        self.refs[5][0, self.chunk, lanes, :] = value.astype(self.refs[5].dtype)


def _mlstm_chunk(views, st_ref, m_ref):
    L = MLSTM_L
    bf16 = jnp.bfloat16
    row = lax.broadcasted_iota(jnp.int32, (L, L), 0)
    col = lax.broadcasted_iota(jnp.int32, (L, L), 1)
    ones_rows = jnp.ones((BF16_SUBLANES, L), bf16)
    pre = []
    for d, view in enumerate(views):
        last = L - 1 if d == 0 else 0
        a8, b8, cm8 = (view.gate(kind) for kind in range(3))
        m8 = m_ref[d]
        c8 = jnp.maximum(cm8, m8)
        c_last = c8[:, last:last + 1]
        m_ref[d] = b8[:, last:last + 1] + c_last
        pre.append(dict(
            c=c8, w=jnp.exp(m8 - c8), floor=jnp.exp(-b8 - c8), wk=jnp.exp(a8 - c_last).astype(bf16),
            decay=jnp.exp(m8 - c_last),
            mask=(row <= col) if d == 0 else (row >= col)))

    for h in range(M_HEADS):
        for d, view in enumerate(views):
            p = pre[d]
            idx = d * M_HEADS + h
            lanes = slice(h * M_HEAD_DIM, (h + 1) * M_HEAD_DIM)
            q = view.q(lanes)
            k = view.k(lanes)
            vt = jnp.concatenate([view.vt(lanes), ones_rows], axis=0)
            st = st_ref[idx]

            e = jnp.exp(jnp.where(p["mask"], view.a_col(h) - p["c"][h:h + 1, :], -jnp.inf))
            pt = (_dot_nt(k, q) * e).astype(bf16)
            num = _dot(vt, pt) + p["w"][h:h + 1, :] * _dot_nt(st.astype(bf16), q)
            den = num[M_HEAD_DIM:M_HEAD_DIM + 1, :]
            view.put_h(lanes, num[:M_HEAD_DIM] / jnp.maximum(jnp.abs(den), p["floor"][h:h + 1, :]))

    for h in range(M_HEADS):
        for d, view in enumerate(views):
            p = pre[d]
            idx = d * M_HEADS + h
            lanes = slice(h * M_HEAD_DIM, (h + 1) * M_HEAD_DIM)
            vt = jnp.concatenate([view.vt(lanes), ones_rows], axis=0)
            st_ref[idx] = p["decay"][h:h + 1, :] * st_ref[idx] + _dot(vt * p["wk"][h:h + 1, :], view.k(lanes))


def _mlstm_grid_kernel(qf_ref, kf_ref, vtf_ref, abcf_ref, atf_ref, qb_ref, kb_ref, vtb_ref, abcb_ref, atb_ref,
                       hf_ref, hb_ref, st_ref, m_ref):
    @pl.when(pl.program_id(1) == 0)
    def _():
        st_ref[...] = jnp.zeros_like(st_ref)
        m_ref[...] = jnp.zeros_like(m_ref)

    _mlstm_chunk([_ChunkView(0, qf_ref, kf_ref, vtf_ref, abcf_ref, atf_ref, hf_ref, 0, 0),
                  _ChunkView(1, qb_ref, kb_ref, vtb_ref, abcb_ref, atb_ref, hb_ref, 0, 0)], st_ref, m_ref)


def _mlstm_seq_kernel(q_ref, k_ref, vt_ref, abc_ref, at_ref, hf_ref, hb_ref, st_ref, m_ref):
    nc = vt_ref.shape[1]
    st_ref[...] = jnp.zeros_like(st_ref)
    m_ref[...] = jnp.zeros_like(m_ref)

    def body(c, carry):
        def view(d, cd, ht_ref):
            return _ChunkView(d, q_ref, k_ref, vt_ref, abc_ref, at_ref, ht_ref,
                              pl.multiple_of(cd * MLSTM_L, MLSTM_L), cd)
        _mlstm_chunk([view(0, c, hf_ref), view(1, nc - 1 - c, hb_ref)], st_ref, m_ref)
        return carry

    lax.fori_loop(0, nc, body, 0)


def _mlstm(proj, v_t, gates):
    B, T, _ = proj.shape
    L = MLSTM_L
    assert T % L == 0
    nc = T // L
    abc, a_t = gates
    out_shape = [jax.ShapeDtypeStruct((B, nc, M_WIDTH, L), jnp.bfloat16)] * 2
    scratch = [pltpu.VMEM((2 * M_HEADS, M_AUG, M_HEAD_DIM), jnp.float32),
               pltpu.VMEM((2, SUBLANES, 1), jnp.float32)]

    if T <= MLSTM_SEQ_T:
        def whole(arr_cols, col):
            return pl.BlockSpec((1, T, arr_cols), lambda b, col=col: (b, 0, col))

        def chunks(rows):
            return pl.BlockSpec((1, nc, rows, L), lambda b: (b, 0, 0, 0))

        return pl.pallas_call(
            _mlstm_seq_kernel,
            grid=(B,),
            in_specs=[whole(M_WIDTH, COL_MQ), whole(M_WIDTH, COL_MK), chunks(M_WIDTH), chunks(3 * N_GATES),
                      whole(LANES, 0)],
            out_specs=[chunks(M_WIDTH), chunks(M_WIDTH)],
            out_shape=out_shape,
            scratch_shapes=scratch,
            compiler_params=pltpu.CompilerParams(
                dimension_semantics=("arbitrary",), vmem_limit_bytes=VMEM_LIMIT_BYTES),
            name="mlstm_seq",
        )(proj, proj, v_t, abc, a_t)

    def tok(width, col, rev):
        return pl.BlockSpec((1, L, width), lambda b, c: (b, nc - 1 - c if rev else c, col))

    def chunk(rows, rev):
        return pl.BlockSpec((1, 1, rows, L), lambda b, c: (b, nc - 1 - c if rev else c, 0, 0))

    def side(rev):
        return [tok(M_WIDTH, COL_MQ, rev), tok(M_WIDTH, COL_MK, rev), chunk(M_WIDTH, rev),
                chunk(3 * N_GATES, rev), tok(LANES, 0, rev)]

    return pl.pallas_call(
        _mlstm_grid_kernel,
        grid=(B, nc),
        in_specs=side(False) + side(True),
        out_specs=[chunk(M_WIDTH, False), chunk(M_WIDTH, True)],
        out_shape=out_shape,
        scratch_shapes=scratch,
        compiler_params=pltpu.CompilerParams(
            dimension_semantics=("arbitrary", "arbitrary"),
            vmem_limit_bytes=VMEM_LIMIT_BYTES),
        name="mlstm",
    )(proj, proj, v_t, abc, a_t, proj, proj, v_t, abc, a_t)


_NA_KB = tuple(int(v) for v in np.clip(np.arange(GRID_W // NA_QCB) * NA_QCB - NA_KW // 2, 0, GRID_W - NA_KCB))
NA_NCB = GRID_W // NA_QCB
NA_MQ = NA_RB * NA_QCB
NA_NK = NA_WIN * NA_KCB


def _na_bias_table(rpb):
    n_dr, n_dc = 2 * NA_MAX_KH - 1, 2 * NA_KW - 1
    i = np.arange(NA_RB)[:, None]
    jr = np.arange(NA_WIN)[None, :]
    row_sel = np.zeros((3, NA_RB, NA_WIN, n_dr + 1), np.float32)
    for rt in range(3):
        off = (0, -NA_MAX_KH // 2, NA_RB - NA_WIN)[rt]
        if rt == 0:
            rs = np.maximum(i - NA_MAX_KH // 2, 0)
        elif rt == 1:
            rs = i - NA_MAX_KH // 2
        else:
            rs = np.minimum(i - NA_MAX_KH // 2, NA_RB - NA_MAX_KH)
        krow = off + jr
        ii, jj = np.nonzero((krow >= rs) & (krow < rs + NA_MAX_KH))
        row_sel[rt, ii, jj, (krow - i)[ii, jj] + NA_MAX_KH - 1] = 1.0
    row_sel[..., n_dr] = 1.0 - row_sel.sum(-1)
    qc = np.arange(NA_QCB)[:, None]
    kc = np.arange(NA_KCB)[None, :]
    col_sel = np.zeros((NA_NCB, NA_QCB, NA_KCB, n_dc + 1), np.float32)
    for n in range(NA_NCB):
        qcol = n * NA_QCB + qc
        kcol = _NA_KB[n] + kc
        cs = np.clip(qcol - NA_KW // 2, 0, GRID_W - NA_KW)
        qq, kk = np.nonzero((kcol >= cs) & (kcol < cs + NA_KW))
        col_sel[n, qq, kk, (kcol - qcol)[qq, kk] + NA_KW - 1] = 1.0
    col_sel[..., n_dc] = 1.0 - col_sel.sum(-1)
    r = jnp.pad(rpb.astype(jnp.float32) * LOG2E, ((0, 0), (0, 1), (0, 1)), constant_values=NEG)
    r = r.reshape(NA_HEADS // NA_HPB, NA_HPB, n_dr + 1, n_dc + 1)
    hi = lax.Precision.HIGHEST
    a = jnp.einsum("ghrc,xijr->gxhijc", r, row_sel, precision=hi)
    t = jnp.einsum("gxhijc,nqkc->gxnhiqjk", a, col_sel, precision=hi)
    return t.reshape(NA_HEADS // NA_HPB, 3 * NA_NCB, NA_HPB * NA_MQ, NA_NK)


def _natten_kernel(q_ref, *refs):
    kv_refs, (z_ref, tab_ref, o_ref, acc_s) = refs[:-4], refs[-4:]
    nrb = pl.num_programs(2) * NA_BPS
    tq = NA_RB * GRID_W
    lane = lax.broadcasted_iota(jnp.int32, (NA_MQ, NA_LANES), 1) // NA_HEAD_DIM

    def keys(xf, n):
        kb = _NA_KB[n]
        return jnp.concatenate([xf[j * GRID_W + kb:j * GRID_W + kb + NA_KCB] for j in range(NA_WIN)],
                               axis=0).astype(jnp.bfloat16)

    for blk in range(NA_BPS):
        rb = pl.program_id(2) * NA_BPS + blk
        rt = jnp.where(rb == 0, 0, jnp.where(rb == nrb - 1, 2, 1))
        kf = kv_refs[2 * blk][0].astype(jnp.float32)
        vf = kv_refs[2 * blk + 1][0].astype(jnp.float32)
        q = q_ref[0, blk * tq:(blk + 1) * tq, :]
        for n in range(NA_NCB):
            qn = jnp.concatenate([q[i * GRID_W + n * NA_QCB:i * GRID_W + (n + 1) * NA_QCB] for i in range(NA_RB)],
                                 axis=0)
            qs = jnp.concatenate([jnp.where(lane == hh, qn, 0.0).astype(jnp.bfloat16) for hh in range(NA_HPB)],
                                 axis=0)
            s = _dot_nt(qs, keys(kf, n)) + tab_ref[0, rt * NA_NCB + n]
            m = jnp.max(s, axis=1, keepdims=True)
            e = jnp.exp2(s - m)
            l = jnp.sum(e, axis=1, keepdims=True)
            o = _dot(e.astype(jnp.bfloat16), keys(vf, n)) / l
            on = jnp.zeros((NA_MQ, NA_LANES), jnp.float32)
            for hh in range(NA_HPB):
                on = jnp.where(lane == hh, o[hh * NA_MQ:(hh + 1) * NA_MQ], on)
            for i in range(NA_RB):
                r0 = blk * tq + i * GRID_W + n * NA_QCB
                acc_s[r0:r0 + NA_QCB, :] = on[i * NA_QCB:(i + 1) * NA_QCB]
    o_ref[0] = (acc_s[...] * _silu(z_ref[0].astype(jnp.float32))).astype(o_ref.dtype)


def _natten(proj, table):
    B, T, _ = proj.shape
    rows = T // GRID_W
    assert T % GRID_W == 0 and rows % (NA_RB * NA_BPS) == 0 and rows >= NA_WIN
    ng = NA_HEADS // NA_HPB
    tq = NA_BPS * NA_RB * GRID_W
    cpt = D_MODEL // NA_LANES

    def win(col, blk):
        def index(g, b, s):
            start = jnp.clip((s * NA_BPS + blk) * NA_RB - NA_MAX_KH // 2, 0, rows - NA_WIN)
            return (b, start * GRID_W, (col * cpt + g) * NA_LANES)
        return pl.BlockSpec((pl.Element(1), pl.Element(NA_WIN * GRID_W), pl.Element(NA_LANES)), index)

    windows = []
    for blk in range(NA_BPS):
        windows += [win(COL_NK, blk), win(COL_NV, blk)]
    return pl.pallas_call(
        _natten_kernel,
        grid=(ng, B, rows // (NA_RB * NA_BPS)),
        in_specs=[pl.BlockSpec((1, tq, NA_LANES), lambda g, b, s: (b, s, COL_NQ * cpt + g))]
        + windows
        + [pl.BlockSpec((1, tq, NA_LANES), lambda g, b, s: (b, s, COL_NZ * cpt + g)),
           pl.BlockSpec((1, 3 * NA_NCB, NA_HPB * NA_MQ, NA_NK), lambda g, b, s: (g, 0, 0, 0))],
        out_specs=pl.BlockSpec((1, tq, NA_LANES), lambda g, b, s: (b, s, g)),
        out_shape=jax.ShapeDtypeStruct((B, T, NA_WIDTH), jnp.bfloat16),
        scratch_shapes=[pltpu.VMEM((tq, NA_LANES), jnp.float32)],
        compiler_params=pltpu.CompilerParams(
            dimension_semantics=("arbitrary", "arbitrary", "arbitrary"),
            vmem_limit_bytes=VMEM_LIMIT_BYTES),
        name="natten",
    )(proj, *([proj] * (2 * NA_BPS)), proj, table)


def _out_kernel(x_ref, hf_ref, hb_ref, o_ref, z_ref, yb_ref, ga_ref, gb_ref, mhw_ref, wda_ref, wdb_ref, wo_ref,
                fnw_ref, y_ref):
    f32 = jnp.float32
    tm = x_ref.shape[1]
    for r0 in range(0, tm, OUT_SUB):
        rows = slice(r0, r0 + OUT_SUB)
        h = hf_ref[0, r0 // OUT_SUB].astype(f32) + hb_ref[0, r0 // OUT_SUB].astype(f32)
        h = h.T * _sigmoid(o_ref[0, rows, :].astype(f32))
        parts = []
        for hd in range(M_HEADS):
            hh = h[:, hd * M_HEAD_DIM:(hd + 1) * M_HEAD_DIM]
            parts.append(hh * lax.rsqrt(jnp.mean(hh * hh, axis=-1, keepdims=True) + EPS))
        hn = jnp.concatenate(parts, axis=-1) * mhw_ref[...]
        ya = (hn * _silu(z_ref[0, rows, :].astype(f32))).astype(jnp.bfloat16)
        merged = (_sigmoid(ga_ref[0, rows, :].astype(f32)) * _dot(ya, wda_ref[...])
                  + _sigmoid(gb_ref[0, rows, :].astype(f32)) * _dot(yb_ref[0, rows, :], wdb_ref[...]))
        y = x_ref[0, rows, :] + _dot(merged.astype(jnp.bfloat16), wo_ref[...])
        y = y * lax.rsqrt(jnp.mean(y * y, axis=-1, keepdims=True) + EPS)
        y_ref[0, rows, :] = y * fnw_ref[...]


def _out(x, h_f, h_b, proj, y_b, mh_w, w_da, w_db, w_o, fn_w):
    B, T, D = x.shape
    tm = min(OUT_TM, T)
    assert T % tm == 0 and OUT_SUB == MLSTM_L

    def tok(col=0):
        return pl.BlockSpec((1, tm, D_MODEL), lambda b, i, col=col: (b, i, col))

    def feat():
        return pl.BlockSpec((1, tm // MLSTM_L, M_WIDTH, MLSTM_L), lambda b, i: (b, i, 0, 0))

    def const(shape):
        return pl.BlockSpec(shape, lambda b, i: (0,) * len(shape))

    return pl.pallas_call(
        _out_kernel,
        grid=(B, T // tm),
        in_specs=[tok(), feat(), feat(), tok(COL_MO), tok(COL_MZ), tok(), tok(COL_GA), tok(COL_GB),
                  const((1, D)), const((D, D)), const((D, D)), const((D, D)), const((1, D))],
        out_specs=tok(),
        out_shape=jax.ShapeDtypeStruct((B, T, D), jnp.float32),
        compiler_params=pltpu.CompilerParams(
            dimension_semantics=("arbitrary", "arbitrary"),
            vmem_limit_bytes=VMEM_LIMIT_BYTES),
        name="out",
    )(x, h_f, h_b, proj, proj, y_b, proj, proj, mh_w, w_da, w_db, w_o, fn_w)


def _prepare(norm_w, w_in, b_gate, conv_w, conv_b, mh_norm_w, rpb, w_down_a, w_down_b, w_out, final_norm_w):
    bf16 = jnp.bfloat16
    g0 = 5 * M_WIDTH
    v0 = 2 * M_WIDTH
    nq0 = g0 + N_GATES
    w_steps = jnp.concatenate([w_in[:, v0:v0 + M_WIDTH].T, w_in[:, v0 + M_WIDTH:g0],
                               w_in[:, nq0:nq0 + NA_WIDTH] * (LOG2E * NA_HEAD_DIM ** -0.5),
                               w_in[:, nq0 + NA_WIDTH:], w_in[:, :v0]], axis=1).astype(bf16)
    w_gt = w_in[:, g0:g0 + N_GATES].T.astype(bf16)
    cw = jnp.pad(conv_w, ((0, SUBLANES - CONV_W), (0, 0)))
    return dict(
        norm_w=norm_w.reshape(1, D_MODEL), w_steps=w_steps, w_gt=w_gt, b_gt=b_gate.reshape(N_GATES, 1),
        conv_w=cw, conv_b=conv_b.reshape(1, 2 * M_WIDTH), mh_w=mh_norm_w.reshape(1, M_WIDTH),
        table=_na_bias_table(rpb), w_da=w_down_a.astype(bf16), w_db=w_down_b.astype(bf16),
        w_o=w_out.astype(bf16), fn_w=final_norm_w.reshape(1, D_MODEL))


def _trunk(x, p):
    proj, v_t, g_t = _proj(x, p["norm_w"], p["w_steps"], p["w_gt"], p["b_gt"], p["conv_w"], p["conv_b"])
    h_f, h_b = _mlstm(proj, v_t, _gates(g_t, min(MLSTM_L, x.shape[1])))
    y_b = _natten(proj, p["table"])
    return _out(x, h_f, h_b, proj, y_b, p["mh_w"], p["w_da"], p["w_db"], p["w_o"], p["fn_w"])


def kernel(x_prompt, x_sample, norm_w, w_in, b_gate, conv_w, conv_b, mh_norm_w, rpb, w_down_a, w_down_b, w_out,
           final_norm_w):
    assert norm_w.shape[0] == 1, "single-layer trunk"
    p = _prepare(norm_w[0], w_in[0], b_gate[0], conv_w[0], conv_b[0], mh_norm_w[0], rpb[0],
                 w_down_a[0], w_down_b[0], w_out[0], final_norm_w)
    return (_trunk(x_prompt, p), _trunk(x_sample, p))
```

```python
import functools

import numpy as np
import jax
import jax.numpy as jnp
from jax import lax
from jax.experimental import pallas as pl
from jax.experimental.pallas import tpu as pltpu

D_MODEL = 1024
GRID_W = 64
M_HEADS = 4
M_HEAD_DIM = 256
M_WIDTH = M_HEADS * M_HEAD_DIM
CONV_W = 5
N_GATES = 4 * M_HEADS
NA_HEADS = 16
NA_HEAD_DIM = 64
NA_WIDTH = NA_HEADS * NA_HEAD_DIM
NA_MAX_KH = 8
NA_KW = 16
NA_QCB = 16
NA_KCB = 32
EPS = 1e-6
NEG = -1e30
LOG2E = 1.4426950408889634

LANES = 128
SUBLANES = 8
BF16_SUBLANES = 16
VMEM_LIMIT_BYTES = 56 * 1024 * 1024

STEP_MV, STEP_MQ, STEP_MK = 0, 9, 10
N_STEPS = 11
COL_MO, COL_MZ, COL_NQ, COL_NK, COL_NV, COL_NZ, COL_GA, COL_GB, COL_MQ, COL_MK = range(10)
D_MAIN = 10 * D_MODEL

PROJ_TM = 2048
HALO = BF16_SUBLANES
CONV_PIECE = 512
NORM_PIECE = 512
MLSTM_L = 256
MLSTM_SEQ_T = 2048
GATES_TT = 2048
M_AUG = M_HEAD_DIM + BF16_SUBLANES
NA_RB = 4
NA_WIN = NA_RB + NA_MAX_KH
NA_BPS = 8
NA_HPB = 4
NA_LANES = NA_HPB * NA_HEAD_DIM
OUT_TM = 1024
OUT_SUB = 256


def _dot(a, b):
    return jnp.dot(a, b, preferred_element_type=jnp.float32)


def _dot_nt(a, b):
    return lax.dot_general(a, b, (((1,), (1,)), ((), ())), preferred_element_type=jnp.float32)


def _sigmoid(x):
    return 1.0 / (1.0 + jnp.exp2(x * -LOG2E))


def _silu(x):
    return x / (1.0 + jnp.exp2(x * -LOG2E))


def _log_sigmoid(x):
    return jnp.minimum(x, 0.0) - jnp.log1p(jnp.exp(-jnp.abs(x)))


def _proj_kernel(x_ref, xp_ref, xn_ref_, nw_ref, w_ref, wgt_ref, bgt_ref, cw_ref, cb_ref,
                 o_ref, vt_ref, gt_ref, xn_s, acc_s):
    i = pl.program_id(1)
    j = pl.program_id(2)
    ni = pl.num_programs(1)
    tm = x_ref.shape[1]
    pad = CONV_W // 2

    def norm(x):
        y = x * lax.rsqrt(jnp.mean(x * x, axis=-1, keepdims=True) + EPS)
        return y * nw_ref[...]

    @pl.when(j == STEP_MV)
    def _():
        prev = jnp.where(i > 0, norm(xp_ref[0]), 0.0).astype(jnp.bfloat16)
        nxt = jnp.where(i < ni - 1, norm(xn_ref_[0]), 0.0).astype(jnp.bfloat16)
        xn_s[0:HALO, :] = prev
        xn_s[HALO + tm:2 * HALO + tm, :] = nxt
        for r0 in range(0, tm, NORM_PIECE):
            cur = norm(x_ref[0, r0:r0 + NORM_PIECE, :]).astype(jnp.bfloat16)
            xn_s[HALO + r0:HALO + r0 + NORM_PIECE, :] = cur
            gt_ref[0, :, r0:r0 + NORM_PIECE] = _dot_nt(wgt_ref[...], cur) + bgt_ref[...]
            vt = _dot_nt(w_ref[...], cur).astype(vt_ref.dtype)
            for cc in range(NORM_PIECE // MLSTM_L):
                vt_ref[0, r0 // MLSTM_L + cc] = vt[:, cc * MLSTM_L:(cc + 1) * MLSTM_L]

    @pl.when((j == STEP_MQ) | (j == STEP_MK))
    def _():
        scale = jnp.where(j == STEP_MK, M_HEAD_DIM ** -0.5, 1.0).astype(o_ref.dtype)
        for c in range(tm // CONV_PIECE):
            r0 = c * CONV_PIECE
            acc_s[c % 2] = _dot(xn_s[r0:r0 + CONV_PIECE + 2 * HALO, :], w_ref[...])
            y = cb_ref[...]
            for t in range(CONV_W):
                y = y + acc_s[c % 2, pl.ds(HALO - pad + t, CONV_PIECE), :] * cw_ref[t:t + 1, :]
            o_ref[0, r0:r0 + CONV_PIECE, :] = _silu(y).astype(o_ref.dtype) * scale

    @pl.when((j > STEP_MV) & (j < STEP_MQ))
    def _():
        o_ref[0] = _dot(xn_s[HALO:HALO + tm, :], w_ref[...]).astype(o_ref.dtype)


def _proj(x, norm_w, w_steps, w_gt, b_gt, conv_w, conv_b):
    B, T, D = x.shape
    tm = min(PROJ_TM, T)
    assert T % tm == 0 and tm % CONV_PIECE == 0 and tm % NORM_PIECE == 0 and NORM_PIECE % MLSTM_L == 0
    ni = T // tm
    hb = tm // HALO
    return pl.pallas_call(
        _proj_kernel,
        grid=(B, ni, N_STEPS),
        in_specs=[
            pl.BlockSpec((1, tm, D), lambda b, i, j: (b, i, 0)),
            pl.BlockSpec((1, HALO, D), lambda b, i, j: (b, jnp.maximum(i * hb - 1, 0), 0)),
            pl.BlockSpec((1, HALO, D), lambda b, i, j: (b, jnp.minimum((i + 1) * hb, T // HALO - 1), 0)),
            pl.BlockSpec((1, D), lambda b, i, j: (0, 0)),
            pl.BlockSpec((D, D_MODEL), lambda b, i, j: (0, j)),
            pl.BlockSpec((N_GATES, D), lambda b, i, j: (0, 0)),
            pl.BlockSpec((N_GATES, 1), lambda b, i, j: (0, 0)),
            pl.BlockSpec((SUBLANES, D_MODEL), lambda b, i, j: (0, jnp.clip(j - STEP_MQ, 0, 1))),
            pl.BlockSpec((1, D_MODEL), lambda b, i, j: (0, jnp.clip(j - STEP_MQ, 0, 1))),
        ],
        out_specs=[
            pl.BlockSpec((1, tm, D_MODEL), lambda b, i, j: (b, i, jnp.maximum(j - 1, 0))),
            pl.BlockSpec((1, tm // MLSTM_L, M_WIDTH, MLSTM_L), lambda b, i, j: (b, i, 0, 0)),
            pl.BlockSpec((1, N_GATES, tm), lambda b, i, j: (b, 0, i)),
        ],
        out_shape=[
            jax.ShapeDtypeStruct((B, T, D_MAIN), jnp.bfloat16),
            jax.ShapeDtypeStruct((B, T // MLSTM_L, M_WIDTH, MLSTM_L), jnp.bfloat16),
            jax.ShapeDtypeStruct((B, N_GATES, T), jnp.float32),
        ],
        scratch_shapes=[
            pltpu.VMEM((tm + 2 * HALO, D), jnp.bfloat16),
            pltpu.VMEM((2, CONV_PIECE + 2 * HALO, D_MODEL), jnp.float32),
        ],
        compiler_params=pltpu.CompilerParams(
            dimension_semantics=("arbitrary", "arbitrary", "arbitrary"),
            vmem_limit_bytes=VMEM_LIMIT_BYTES),
        name="proj",
    )(x, x, x, norm_w, w_steps, w_gt, b_gt, conv_w, conv_b)


def _scan_lanes(x, pos, L, d, op, fill):
    n = x.shape[1]
    sh = 1
    while sh < L:
        if d == 0:
            x = op(x, jnp.where(pos >= sh, pltpu.roll(x, sh, axis=1), fill))
        else:
            x = op(x, jnp.where(pos < L - sh, pltpu.roll(x, n - sh, axis=1), fill))
        sh *= 2
    return x


def _gates_kernel(L, gt_ref, abc_ref, at_ref):
    n = gt_ref.shape[2]
    pos = lax.broadcasted_iota(jnp.int32, (SUBLANES, n), 1) % L
    a_rows = []
    for d in range(2):
        rows = slice(d * SUBLANES, (d + 1) * SUBLANES)
        g8 = gt_ref[0, rows, :]
        b8 = _scan_lanes(_log_sigmoid(g8), pos, L, d, jnp.add, 0.0)
        b8 = pltpu.roll(b8, M_HEADS, axis=0)
        a8 = g8 - b8
        cm8 = _scan_lanes(a8, pos, L, d, jnp.maximum, -jnp.inf)
        for kind, val in enumerate((a8, b8, cm8)):
            for cc in range(n // L):
                abc_ref[0, cc, kind * N_GATES + d * SUBLANES:kind * N_GATES + (d + 1) * SUBLANES, :] = (
                    val[:, cc * L:(cc + 1) * L])
        a_rows.append(a8)
    pad = jnp.zeros((LANES - 2 * SUBLANES, n), jnp.float32)
    at_ref[0] = jnp.concatenate(a_rows + [pad], axis=0).T


def _gates(g_t, L):
    B, _, T = g_t.shape
    tg = min(GATES_TT, T)
    assert T % tg == 0 and tg % L == 0
    return pl.pallas_call(
        functools.partial(_gates_kernel, L),
        grid=(B, T // tg),
        in_specs=[pl.BlockSpec((1, N_GATES, tg), lambda b, i: (b, 0, i))],
        out_specs=[pl.BlockSpec((1, tg // L, 3 * N_GATES, L), lambda b, i: (b, i, 0, 0)),
                   pl.BlockSpec((1, tg, LANES), lambda b, i: (b, i, 0))],
        out_shape=[jax.ShapeDtypeStruct((B, T // L, 3 * N_GATES, L), jnp.float32),
                   jax.ShapeDtypeStruct((B, T, LANES), jnp.float32)],
        compiler_params=pltpu.CompilerParams(
            dimension_semantics=("arbitrary", "arbitrary"),
            vmem_limit_bytes=VMEM_LIMIT_BYTES),
        name="gates",
    )(g_t)


class _ChunkView:
    def __init__(self, d, q_ref, k_ref, vt_ref, abc_ref, at_ref, ht_ref, row0, chunk):
        self.d, self.refs, self.row0, self.chunk = d, (q_ref, k_ref, vt_ref, abc_ref, at_ref, ht_ref), row0, chunk

    def _rows(self):
        return pl.ds(self.row0, MLSTM_L)

    def q(self, lanes):
        return self.refs[0][0, self._rows(), lanes]

    def k(self, lanes):
        return self.refs[1][0, self._rows(), lanes]

    def vt(self, lanes):
        return self.refs[2][0, self.chunk, lanes, :]

    def gate(self, kind):
        r0 = kind * N_GATES + self.d * SUBLANES
        return self.refs[3][0, self.chunk, r0:r0 + SUBLANES, :]

    def a_col(self, h):
        return self.refs[4][0, self._rows(), self.d * SUBLANES + h:self.d * SUBLANES + h + 1]

    def put_h(self, lanes, value):
        self.refs[5][0, self.chunk, lanes, :] = value.astype(self.refs[5].dtype)


def _mlstm_chunk(views, st_ref, m_ref):
    L = MLSTM_L
    bf16 = jnp.bfloat16
    row = lax.broadcasted_iota(jnp.int32, (L, L), 0)
    col = lax.broadcasted_iota(jnp.int32, (L, L), 1)
    ones_rows = jnp.ones((BF16_SUBLANES, L), bf16)
    pre = []
    for d, view in enumerate(views):
        last = L - 1 if d == 0 else 0
        a8, b8, cm8 = (view.gate(kind) for kind in range(3))
        m8 = m_ref[d]
        c8 = jnp.maximum(cm8, m8)
        c_last = c8[:, last:last + 1]
        m_ref[d] = b8[:, last:last + 1] + c_last
        pre.append(dict(
            c=c8, w=jnp.exp(m8 - c8), floor=jnp.exp(-b8 - c8), wk=jnp.exp(a8 - c_last).astype(bf16),
            decay=jnp.exp(m8 - c_last),
            mask=(row <= col) if d == 0 else (row >= col)))

    for h in range(M_HEADS):
        for d, view in enumerate(views):
            p = pre[d]
            idx = d * M_HEADS + h
            lanes = slice(h * M_HEAD_DIM, (h + 1) * M_HEAD_DIM)
            q = view.q(lanes)
            k = view.k(lanes)
            vt = jnp.concatenate([view.vt(lanes), ones_rows], axis=0)
            st = st_ref[idx]

            e = jnp.exp(jnp.where(p["mask"], view.a_col(h) - p["c"][h:h + 1, :], -jnp.inf))
            pt = (_dot_nt(k, q) * e).astype(bf16)
            num = _dot(vt, pt) + p["w"][h:h + 1, :] * _dot_nt(st.astype(bf16), q)
            den = num[M_HEAD_DIM:M_HEAD_DIM + 1, :]
            view.put_h(lanes, num[:M_HEAD_DIM] / jnp.maximum(jnp.abs(den), p["floor"][h:h + 1, :]))

    for h in range(M_HEADS):
        for d, view in enumerate(views):
            p = pre[d]
            idx = d * M_HEADS + h
            lanes = slice(h * M_HEAD_DIM, (h + 1) * M_HEAD_DIM)
            vt = jnp.concatenate([view.vt(lanes), ones_rows], axis=0)
            st_ref[idx] = p["decay"][h:h + 1, :] * st_ref[idx] + _dot(vt * p["wk"][h:h + 1, :], view.k(lanes))


def _mlstm_grid_kernel(qf_ref, kf_ref, vtf_ref, abcf_ref, atf_ref, qb_ref, kb_ref, vtb_ref, abcb_ref, atb_ref,
                       hf_ref, hb_ref, st_ref, m_ref):
    @pl.when(pl.program_id(1) == 0)
    def _():
        st_ref[...] = jnp.zeros_like(st_ref)
        m_ref[...] = jnp.zeros_like(m_ref)

    _mlstm_chunk([_ChunkView(0, qf_ref, kf_ref, vtf_ref, abcf_ref, atf_ref, hf_ref, 0, 0),
                  _ChunkView(1, qb_ref, kb_ref, vtb_ref, abcb_ref, atb_ref, hb_ref, 0, 0)], st_ref, m_ref)


def _mlstm_seq_kernel(q_ref, k_ref, vt_ref, abc_ref, at_ref, hf_ref, hb_ref, st_ref, m_ref):
    nc = vt_ref.shape[1]
    st_ref[...] = jnp.zeros_like(st_ref)
    m_ref[...] = jnp.zeros_like(m_ref)

    def body(c, carry):
        def view(d, cd, ht_ref):
            return _ChunkView(d, q_ref, k_ref, vt_ref, abc_ref, at_ref, ht_ref,
                              pl.multiple_of(cd * MLSTM_L, MLSTM_L), cd)
        _mlstm_chunk([view(0, c, hf_ref), view(1, nc - 1 - c, hb_ref)], st_ref, m_ref)
        return carry

    lax.fori_loop(0, nc, body, 0)


def _mlstm(proj, v_t, gates):
    B, T, _ = proj.shape
    L = MLSTM_L
    assert T % L == 0
    nc = T // L
    abc, a_t = gates
    out_shape = [jax.ShapeDtypeStruct((B, nc, M_WIDTH, L), jnp.bfloat16)] * 2
    scratch = [pltpu.VMEM((2 * M_HEADS, M_AUG, M_HEAD_DIM), jnp.float32),
               pltpu.VMEM((2, SUBLANES, 1), jnp.float32)]

    if T <= MLSTM_SEQ_T:
        def whole(arr_cols, col):
            return pl.BlockSpec((1, T, arr_cols), lambda b, col=col: (b, 0, col))

        def chunks(rows):
            return pl.BlockSpec((1, nc, rows, L), lambda b: (b, 0, 0, 0))

        return pl.pallas_call(
            _mlstm_seq_kernel,
            grid=(B,),
            in_specs=[whole(M_WIDTH, COL_MQ), whole(M_WIDTH, COL_MK), chunks(M_WIDTH), chunks(3 * N_GATES),
                      whole(LANES, 0)],
            out_specs=[chunks(M_WIDTH), chunks(M_WIDTH)],
            out_shape=out_shape,
            scratch_shapes=scratch,
            compiler_params=pltpu.CompilerParams(
                dimension_semantics=("arbitrary",), vmem_limit_bytes=VMEM_LIMIT_BYTES),
            name="mlstm_seq",
        )(proj, proj, v_t, abc, a_t)

    def tok(width, col, rev):
        return pl.BlockSpec((1, L, width), lambda b, c: (b, nc - 1 - c if rev else c, col))

    def chunk(rows, rev):
        return pl.BlockSpec((1, 1, rows, L), lambda b, c: (b, nc - 1 - c if rev else c, 0, 0))

    def side(rev):
        return [tok(M_WIDTH, COL_MQ, rev), tok(M_WIDTH, COL_MK, rev), chunk(M_WIDTH, rev),
                chunk(3 * N_GATES, rev), tok(LANES, 0, rev)]

    return pl.pallas_call(
        _mlstm_grid_kernel,
        grid=(B, nc),
        in_specs=side(False) + side(True),
        out_specs=[chunk(M_WIDTH, False), chunk(M_WIDTH, True)],
        out_shape=out_shape,
        scratch_shapes=scratch,
        compiler_params=pltpu.CompilerParams(
            dimension_semantics=("arbitrary", "arbitrary"),
            vmem_limit_bytes=VMEM_LIMIT_BYTES),
        name="mlstm",
    )(proj, proj, v_t, abc, a_t, proj, proj, v_t, abc, a_t)


_NA_KB = tuple(int(v) for v in np.clip(np.arange(GRID_W // NA_QCB) * NA_QCB - NA_KW // 2, 0, GRID_W - NA_KCB))
NA_NCB = GRID_W // NA_QCB
NA_MQ = NA_RB * NA_QCB
NA_NK = NA_WIN * NA_KCB


def _na_bias_table(rpb):
    n_dr, n_dc = 2 * NA_MAX_KH - 1, 2 * NA_KW - 1
    i = np.arange(NA_RB)[:, None]
    jr = np.arange(NA_WIN)[None, :]
    row_sel = np.zeros((3, NA_RB, NA_WIN, n_dr + 1), np.float32)
    for rt in range(3):
        off = (0, -NA_MAX_KH // 2, NA_RB - NA_WIN)[rt]
        if rt == 0:
            rs = np.maximum(i - NA_MAX_KH // 2, 0)
        elif rt == 1:
            rs = i - NA_MAX_KH // 2
        else:
            rs = np.minimum(i - NA_MAX_KH // 2, NA_RB - NA_MAX_KH)
        krow = off + jr
        ii, jj = np.nonzero((krow >= rs) & (krow < rs + NA_MAX_KH))
        row_sel[rt, ii, jj, (krow - i)[ii, jj] + NA_MAX_KH - 1] = 1.0
    row_sel[..., n_dr] = 1.0 - row_sel.sum(-1)
    qc = np.arange(NA_QCB)[:, None]
    kc = np.arange(NA_KCB)[None, :]
    col_sel = np.zeros((NA_NCB, NA_QCB, NA_KCB, n_dc + 1), np.float32)
    for n in range(NA_NCB):
        qcol = n * NA_QCB + qc
        kcol = _NA_KB[n] + kc
        cs = np.clip(qcol - NA_KW // 2, 0, GRID_W - NA_KW)
        qq, kk = np.nonzero((kcol >= cs) & (kcol < cs + NA_KW))
        col_sel[n, qq, kk, (kcol - qcol)[qq, kk] + NA_KW - 1] = 1.0
    col_sel[..., n_dc] = 1.0 - col_sel.sum(-1)
    r = jnp.pad(rpb.astype(jnp.float32) * LOG2E, ((0, 0), (0, 1), (0, 1)), constant_values=NEG)
    r = r.reshape(NA_HEADS // NA_HPB, NA_HPB, n_dr + 1, n_dc + 1)
    hi = lax.Precision.HIGHEST
    a = jnp.einsum("ghrc,xijr->gxhijc", r, row_sel, precision=hi)
    t = jnp.einsum("gxhijc,nqkc->gxnhiqjk", a, col_sel, precision=hi)
    return t.reshape(NA_HEADS // NA_HPB, 3 * NA_NCB, NA_HPB * NA_MQ, NA_NK)


def _natten_kernel(q_ref, *refs):
    kv_refs, (z_ref, tab_ref, o_ref, acc_s) = refs[:-4], refs[-4:]
    nrb = pl.num_programs(2) * NA_BPS
    tq = NA_RB * GRID_W
    lane = lax.broadcasted_iota(jnp.int32, (NA_MQ, NA_LANES), 1) // NA_HEAD_DIM

    def keys(xf, n):
        kb = _NA_KB[n]
        return jnp.concatenate([xf[j * GRID_W + kb:j * GRID_W + kb + NA_KCB] for j in range(NA_WIN)],
                               axis=0).astype(jnp.bfloat16)

    for blk in range(NA_BPS):
        rb = pl.program_id(2) * NA_BPS + blk
        rt = jnp.where(rb == 0, 0, jnp.where(rb == nrb - 1, 2, 1))
        kf = kv_refs[2 * blk][0].astype(jnp.float32)
        vf = kv_refs[2 * blk + 1][0].astype(jnp.float32)
        q = q_ref[0, blk * tq:(blk + 1) * tq, :]
        for n in range(NA_NCB):
            qn = jnp.concatenate([q[i * GRID_W + n * NA_QCB:i * GRID_W + (n + 1) * NA_QCB] for i in range(NA_RB)],
                                 axis=0)
            qs = jnp.concatenate([jnp.where(lane == hh, qn, 0.0).astype(jnp.bfloat16) for hh in range(NA_HPB)],
                                 axis=0)
            s = _dot_nt(qs, keys(kf, n)) + tab_ref[0, rt * NA_NCB + n]
            m = jnp.max(s, axis=1, keepdims=True)
            e = jnp.exp2(s - m)
            l = jnp.sum(e, axis=1, keepdims=True)
            o = _dot(e.astype(jnp.bfloat16), keys(vf, n)) / l
            on = jnp.zeros((NA_MQ, NA_LANES), jnp.float32)
            for hh in range(NA_HPB):
                on = jnp.where(lane == hh, o[hh * NA_MQ:(hh + 1) * NA_MQ], on)
            for i in range(NA_RB):
                r0 = blk * tq + i * GRID_W + n * NA_QCB
                acc_s[r0:r0 + NA_QCB, :] = on[i * NA_QCB:(i + 1) * NA_QCB]
    o_ref[0] = (acc_s[...] * _silu(z_ref[0].astype(jnp.float32))).astype(o_ref.dtype)


def _natten(proj, table):
    B, T, _ = proj.shape
    rows = T // GRID_W
    assert T % GRID_W == 0 and rows % (NA_RB * NA_BPS) == 0 and rows >= NA_WIN
    ng = NA_HEADS // NA_HPB
    tq = NA_BPS * NA_RB * GRID_W
    cpt = D_MODEL // NA_LANES

    def win(col, blk):
        def index(g, b, s):
            start = jnp.clip((s * NA_BPS + blk) * NA_RB - NA_MAX_KH // 2, 0, rows - NA_WIN)
            return (b, start * GRID_W, (col * cpt + g) * NA_LANES)
        return pl.BlockSpec((pl.Element(1), pl.Element(NA_WIN * GRID_W), pl.Element(NA_LANES)), index)

    windows = []
    for blk in range(NA_BPS):
        windows += [win(COL_NK, blk), win(COL_NV, blk)]
    return pl.pallas_call(
        _natten_kernel,
        grid=(ng, B, rows // (NA_RB * NA_BPS)),
        in_specs=[pl.BlockSpec((1, tq, NA_LANES), lambda g, b, s: (b, s, COL_NQ * cpt + g))]
        + windows
        + [pl.BlockSpec((1, tq, NA_LANES), lambda g, b, s: (b, s, COL_NZ * cpt + g)),
           pl.BlockSpec((1, 3 * NA_NCB, NA_HPB * NA_MQ, NA_NK), lambda g, b, s: (g, 0, 0, 0))],
        out_specs=pl.BlockSpec((1, tq, NA_LANES), lambda g, b, s: (b, s, g)),
        out_shape=jax.ShapeDtypeStruct((B, T, NA_WIDTH), jnp.bfloat16),
        scratch_shapes=[pltpu.VMEM((tq, NA_LANES), jnp.float32)],
        compiler_params=pltpu.CompilerParams(
            dimension_semantics=("arbitrary", "arbitrary", "arbitrary"),
            vmem_limit_bytes=VMEM_LIMIT_BYTES),
        name="natten",
    )(proj, *([proj] * (2 * NA_BPS)), proj, table)


def _out_kernel(x_ref, hf_ref, hb_ref, o_ref, z_ref, yb_ref, ga_ref, gb_ref, mhw_ref, wda_ref, wdb_ref, wo_ref,
                fnw_ref, y_ref):
    f32 = jnp.float32
    tm = x_ref.shape[1]
    for r0 in range(0, tm, OUT_SUB):
        rows = slice(r0, r0 + OUT_SUB)
        h = hf_ref[0, r0 // OUT_SUB].astype(f32) + hb_ref[0, r0 // OUT_SUB].astype(f32)
        h = h.T * _sigmoid(o_ref[0, rows, :].astype(f32))
        parts = []
        for hd in range(M_HEADS):
            hh = h[:, hd * M_HEAD_DIM:(hd + 1) * M_HEAD_DIM]
            parts.append(hh * lax.rsqrt(jnp.mean(hh * hh, axis=-1, keepdims=True) + EPS))
        hn = jnp.concatenate(parts, axis=-1) * mhw_ref[...]
        ya = (hn * _silu(z_ref[0, rows, :].astype(f32))).astype(jnp.bfloat16)
        merged = (_sigmoid(ga_ref[0, rows, :].astype(f32)) * _dot(ya, wda_ref[...])
                  + _sigmoid(gb_ref[0, rows, :].astype(f32)) * _dot(yb_ref[0, rows, :], wdb_ref[...]))
        y = x_ref[0, rows, :] + _dot(merged.astype(jnp.bfloat16), wo_ref[...])
        y = y * lax.rsqrt(jnp.mean(y * y, axis=-1, keepdims=True) + EPS)
        y_ref[0, rows, :] = y * fnw_ref[...]


def _out(x, h_f, h_b, proj, y_b, mh_w, w_da, w_db, w_o, fn_w):
    B, T, D = x.shape
    tm = min(OUT_TM, T)
    assert T % tm == 0 and OUT_SUB == MLSTM_L

    def tok(col=0):
        return pl.BlockSpec((1, tm, D_MODEL), lambda b, i, col=col: (b, i, col))

    def feat():
        return pl.BlockSpec((1, tm // MLSTM_L, M_WIDTH, MLSTM_L), lambda b, i: (b, i, 0, 0))

    def const(shape):
        return pl.BlockSpec(shape, lambda b, i: (0,) * len(shape), pipeline_mode=pl.Buffered(1))

    return pl.pallas_call(
        _out_kernel,
        grid=(B, T // tm),
        in_specs=[tok(), feat(), feat(), tok(COL_MO), tok(COL_MZ), tok(), tok(COL_GA), tok(COL_GB),
                  const((1, D)), const((D, D)), const((D, D)), const((D, D)), const((1, D))],
        out_specs=tok(),
        out_shape=jax.ShapeDtypeStruct((B, T, D), jnp.float32),
        compiler_params=pltpu.CompilerParams(
            dimension_semantics=("arbitrary", "arbitrary"),
            vmem_limit_bytes=VMEM_LIMIT_BYTES),
        name="out",
    )(x, h_f, h_b, proj, proj, y_b, proj, proj, mh_w, w_da, w_db, w_o, fn_w)


def _prepare(norm_w, w_in, b_gate, conv_w, conv_b, mh_norm_w, rpb, w_down_a, w_down_b, w_out, final_norm_w):
    bf16 = jnp.bfloat16
    g0 = 5 * M_WIDTH
    v0 = 2 * M_WIDTH
    nq0 = g0 + N_GATES
    w_steps = jnp.concatenate([w_in[:, v0:v0 + M_WIDTH].T, w_in[:, v0 + M_WIDTH:g0],
                               w_in[:, nq0:nq0 + NA_WIDTH] * (LOG2E * NA_HEAD_DIM ** -0.5),
                               w_in[:, nq0 + NA_WIDTH:], w_in[:, :v0]], axis=1).astype(bf16)
    w_gt = w_in[:, g0:g0 + N_GATES].T.astype(bf16)
    cw = jnp.pad(conv_w, ((0, SUBLANES - CONV_W), (0, 0)))
    return dict(
        norm_w=norm_w.reshape(1, D_MODEL), w_steps=w_steps, w_gt=w_gt, b_gt=b_gate.reshape(N_GATES, 1),
        conv_w=cw, conv_b=conv_b.reshape(1, 2 * M_WIDTH), mh_w=mh_norm_w.reshape(1, M_WIDTH),
        table=_na_bias_table(rpb), w_da=w_down_a.astype(bf16), w_db=w_down_b.astype(bf16),
        w_o=w_out.astype(bf16), fn_w=final_norm_w.reshape(1, D_MODEL))


def _trunk(x, p):
    proj, v_t, g_t = _proj(x, p["norm_w"], p["w_steps"], p["w_gt"], p["b_gt"], p["conv_w"], p["conv_b"])
    h_f, h_b = _mlstm(proj, v_t, _gates(g_t, min(MLSTM_L, x.shape[1])))
    y_b = _natten(proj, p["table"])
    return _out(x, h_f, h_b, proj, y_b, p["mh_w"], p["w_da"], p["w_db"], p["w_o"], p["fn_w"])


def kernel(x_prompt, x_sample, norm_w, w_in, b_gate, conv_w, conv_b, mh_norm_w, rpb, w_down_a, w_down_b, w_out,
           final_norm_w):
    assert norm_w.shape[0] == 1, "single-layer trunk"
    p = _prepare(norm_w[0], w_in[0], b_gate[0], conv_w[0], conv_b[0], mh_norm_w[0], rpb[0],
                 w_down_a[0], w_down_b[0], w_out[0], final_norm_w)
    return (_trunk(x_prompt, p), _trunk(x_sample, p))
```
